```python
import jax, jax.numpy as jnp
from jax import lax
import numpy as np

D_MODEL = 1024
BATCH = 2
SEQ = 8192
DEPTH = 4

CTX_LEN = 256
GRID_W = 64
HGRN_HEADS = 8
HGRN_HEAD_DIM = D_MODEL // HGRN_HEADS
CHUNK = 64
CONV_WIDTH = 3
FFN_HIDDEN = ((8 * D_MODEL + 3 * 256 - 1) // (3 * 256)) * 256
N_HGRN_LAYERS = (DEPTH + 1) // 2
N_CONV_LAYERS = DEPTH // 2
N_MOD = 6
EPS = 1e-6
F_FLOOR = 1e-6

kernel_name = "hybrid_hgrn2_shortconv_dit_trunk"


def _rmsnorm(x, w):
    xf = x.astype(jnp.float32)
    y = xf * lax.rsqrt(jnp.mean(xf * xf, axis=-1, keepdims=True) + EPS)
    return (y * w.astype(jnp.float32)).astype(x.dtype)


def _modulate(h, shift, scale):
    return h * (1 + scale) + shift


def _heads(t):
    return t.reshape(t.shape[:-1] + (HGRN_HEADS, HGRN_HEAD_DIM))


def _chunk_scan(q, k, v, log_f, s0):
    b, l, h, _ = q.shape
    dv = v.shape[-1]
    n = l // CHUNK

    def to_chunks(t):
        return t.astype(jnp.float32).reshape(b, n, CHUNK, h, t.shape[-1]).transpose(1, 0, 3, 2, 4)

    mask = jnp.tril(jnp.ones((CHUNK, CHUNK), dtype=bool))[:, :, None]

    def step(s, inp):
        qc, kc, vc, gc = inp
        cum = jnp.cumsum(gc, axis=2)
        o_inter = jnp.einsum("bhtk,bhkv->bhtv", qc * jnp.exp(cum), s)
        diff = cum[:, :, :, None, :] - cum[:, :, None, :, :]
        decay = jnp.where(mask, jnp.exp(jnp.where(mask, diff, 0.0)), 0.0)
        scores = jnp.einsum("bhtk,bhsk,bhtsk->bhts", qc, kc, decay)
        o = o_inter + jnp.einsum("bhts,bhsv->bhtv", scores, vc)
        last = cum[:, :, -1:, :]
        s_new = jnp.exp(last[:, :, 0, :])[..., None] * s + jnp.einsum(
            "bhsk,bhsv->bhkv", kc * jnp.exp(last - cum), vc)
        return s_new, o

    s_fin, o = lax.scan(step, s0, (to_chunks(q), to_chunks(k), to_chunks(v), to_chunks(log_f)))
    return o.transpose(1, 0, 3, 2, 4).reshape(b, l, h, dv), s_fin


def _forget_gates(z, lb):
    zf = z.astype(jnp.float32)
    f = lb + (1.0 - lb) * jax.nn.sigmoid(zf)
    log_f = jnp.log(jnp.maximum(f, F_FLOOR))
    k = (1.0 - lb) * jax.nn.sigmoid(-zf)
    return _heads(k), _heads(log_f)


def _direction(qc, vc, zc, ql, vl, zl, lb, s0):
    kc, lc = _forget_gates(zc, lb)
    kl, ll = _forget_gates(zl, lb)
    oc, s_ctx = _chunk_scan(qc, kc, vc, lc, s0)
    ol, _ = _chunk_scan(ql, kl, vl, ll, s_ctx)
    return oc, ol


def _readout(o, g, g_norm, w_out, dtype):
    o = o * lax.rsqrt(jnp.mean(o * o, axis=-1, keepdims=True) + EPS) * g_norm.astype(jnp.float32)
    o = o * jax.nn.silu(_heads(g.astype(jnp.float32)))
    return o.reshape(o.shape[:-2] + (HGRN_HEADS * HGRN_HEAD_DIM,)).astype(dtype) @ w_out


def _hgrn2_mixer(h, hc, w_in, w_out, g_norm, lb, need_ctx_out):
    lb = lb.astype(jnp.float32)
    q_scale = HGRN_HEAD_DIM ** -0.5
    ql, vl, zfl, zbl, gl = jnp.split(h @ w_in, 5, axis=-1)
    qc, vc, zfc, zbc, gc = jnp.split(hc @ w_in, 5, axis=-1)
    ql, qc = _heads(ql) * q_scale, _heads(qc) * q_scale
    vl, vc = _heads(vl), _heads(vc)
    s0 = jnp.zeros((h.shape[0], HGRN_HEADS, HGRN_HEAD_DIM, HGRN_HEAD_DIM), jnp.float32)

    def flip(t):
        return jnp.flip(t, axis=1)

    oc_f, ol_f = _direction(qc, vc, zfc, ql, vl, zfl, lb, s0)
    oc_b, ol_b = _direction(flip(qc), flip(vc), flip(zbc), flip(ql), flip(vl), flip(zbl), lb, s0)
    y = _readout(ol_f + flip(ol_b), gl, g_norm, w_out, h.dtype)
    yc = _readout(oc_f + flip(oc_b), gc, g_norm, w_out, hc.dtype) if need_ctx_out else None
    return y, yc


def _dwconv(u, w):
    pad = CONV_WIDTH // 2
    length = u.shape[1]
    up = jnp.pad(u, ((0, 0), (pad, CONV_WIDTH - 1 - pad), (0, 0)))
    out = up[:, 0:length] * w[0]
    for j in range(1, CONV_WIDTH):
        out = out + up[:, j:j + length] * w[j]
    return out


def _short_conv_mixer(h, w_in, w_conv, w_out, rows):
    b, l, d = h.shape
    gate_b, gate_c, xin = jnp.split(h @ w_in, 3, axis=-1)
    u = gate_c * xin
    if rows is None:
        y = _dwconv(u, w_conv)
    else:
        y = _dwconv(u.reshape(b * rows, GRID_W, d), w_conv).reshape(b, l, d)
    return (gate_b * y) @ w_out


def _swiglu(h, w_in, w_out):
    gate, up = jnp.split(h @ w_in, 2, axis=-1)
    return (jax.nn.silu(gate) * up) @ w_out


def setup_inputs(seed: int = 0) -> dict:
    key = jax.random.key(seed)
    ks = jax.random.split(key, 16)
    d = D_MODEL

    def nrm(k, shape, scale):
        return scale * jax.random.normal(k, shape, jnp.float32)

    return {
        "x": nrm(ks[0], (BATCH, SEQ, d), 1.0),
        "c": nrm(ks[1], (BATCH, d), 1.0),
        "ctx": nrm(ks[2], (BATCH, CTX_LEN, d), 1.0),
        "c_ctx": nrm(ks[3], (d,), 1.0),
        "ada_w": nrm(ks[4], (DEPTH, d, N_MOD * d), 0.5 * d ** -0.5),
        "ada_b": nrm(ks[5], (DEPTH, N_MOD * d), 0.01),
        "norm_w": 1.0 + nrm(ks[6], (DEPTH, 4, d), 0.05),
        "hgrn_w_in": nrm(ks[7], (N_HGRN_LAYERS, d, 5 * d), d ** -0.5),
        "hgrn_w_out": nrm(ks[8], (N_HGRN_LAYERS, d, d), d ** -0.5),
        "hgrn_gnorm": 1.0 + nrm(ks[9], (N_HGRN_LAYERS, HGRN_HEAD_DIM), 0.05),
        "hgrn_lb": 1.0 + nrm(ks[10], (N_HGRN_LAYERS, HGRN_HEADS * HGRN_HEAD_DIM), 0.5),
        "conv_w_in": nrm(ks[11], (N_CONV_LAYERS, d, 3 * d), d ** -0.5),
        "conv_w": nrm(ks[12], (N_CONV_LAYERS, CONV_WIDTH, d), CONV_WIDTH ** -0.5),
        "conv_w_out": nrm(ks[13], (N_CONV_LAYERS, d, d), d ** -0.5),
        "ffn_w_in": nrm(ks[14], (DEPTH, d, 2 * FFN_HIDDEN), d ** -0.5),
        "ffn_w_out": nrm(ks[15], (DEPTH, FFN_HIDDEN, d), FFN_HIDDEN ** -0.5),
    }


def reference(x, c, ctx, c_ctx, ada_w, ada_b, norm_w, hgrn_w_in, hgrn_w_out, hgrn_gnorm, hgrn_lb,
              conv_w_in, conv_w, conv_w_out, ffn_w_in, ffn_w_out):
    rows = x.shape[1] // GRID_W
    soft = jax.nn.softmax(hgrn_lb.astype(jnp.float32), axis=0)
    lower_bounds = jnp.cumsum(soft, axis=0) - soft[0]
    silu_c = jax.nn.silu(c)
    silu_cc = jax.nn.silu(c_ctx)
    for l in range(DEPTH):
        j = l // 2
        is_hgrn = l % 2 == 0
        need_ctx_out = l < DEPTH - 1
        mod = silu_c @ ada_w[l] + ada_b[l]
        sh1, sc1, g1, sh2, sc2, g2 = [m[:, None, :] for m in jnp.split(mod, N_MOD, axis=-1)]
        h = _modulate(_rmsnorm(x, norm_w[l, 0]), sh1, sc1)
        if is_hgrn or need_ctx_out:
            cmod = silu_cc @ ada_w[l] + ada_b[l]
            csh1, csc1, cg1, csh2, csc2, cg2 = jnp.split(cmod, N_MOD, axis=-1)
            hc = _modulate(_rmsnorm(ctx, norm_w[l, 0]), csh1, csc1)
        if is_hgrn:
            y, yc = _hgrn2_mixer(h, hc, hgrn_w_in[j], hgrn_w_out[j], hgrn_gnorm[j],
                                 lower_bounds[j], need_ctx_out)
        else:
            y = _short_conv_mixer(h, conv_w_in[j], conv_w[j], conv_w_out[j], rows)
            yc = _short_conv_mixer(hc, conv_w_in[j], conv_w[j], conv_w_out[j], None) if need_ctx_out else None
        x = x + g1 * _rmsnorm(y, norm_w[l, 1])
        hf = _modulate(_rmsnorm(x, norm_w[l, 2]), sh2, sc2)
        x = x + g2 * _rmsnorm(_swiglu(hf, ffn_w_in[l], ffn_w_out[l]), norm_w[l, 3])
        if need_ctx_out:
            ctx = ctx + cg1 * _rmsnorm(yc, norm_w[l, 1])
            hfc = _modulate(_rmsnorm(ctx, norm_w[l, 2]), csh2, csc2)
            ctx = ctx + cg2 * _rmsnorm(_swiglu(hfc, ffn_w_in[l], ffn_w_out[l]), norm_w[l, 3])
    return x
```

```python
import functools

import numpy as np
import jax
import jax.numpy as jnp
from jax import lax
from jax.experimental import pallas as pl
from jax.experimental.pallas import tpu as pltpu

D_MODEL = 1024
CTX_LEN = 256
GRID_W = 64
HEADS = 8
HEAD_DIM = D_MODEL // HEADS
CONV_WIDTH = 3
N_MOD = 6
EPS = 1e-6
F_FLOOR = 1e-6

SCAN_CHUNK = 64
SCAN_BASE = 8
ROW_TILE = 768
MOD_ROWS = 8
VMEM_LIMIT = 56 * 1024 * 1024

F32 = jnp.float32
BF16 = jnp.bfloat16


def _bf16_parts(x, n):
    parts = []
    r = x
    for i in range(n):
        p = r.astype(BF16)
        parts.append(p)
        if i + 1 < n:
            r = r - p.astype(F32)
    return parts


def _dot(a, b):
    return jnp.dot(a, b, preferred_element_type=F32)


def _rms(x):
    return x * lax.rsqrt(jnp.mean(x * x, axis=-1, keepdims=True) + EPS)


def _mod_rows(mod_ref, batch, row0, rows, idx):
    cols = slice(idx * D_MODEL, (idx + 1) * D_MODEL)
    m_b = mod_ref[pl.ds(batch, 1), cols]
    m_c = mod_ref[pl.ds(MOD_ROWS - 1, 1), cols]
    pos = row0 + lax.broadcasted_iota(jnp.int32, (rows, 1), 0)
    return jnp.where(pos < CTX_LEN, m_c, m_b)


def _prenorm(x, nw, mod_ref, batch, row0, shift_idx):
    rows = x.shape[0]
    shift = _mod_rows(mod_ref, batch, row0, rows, shift_idx)
    scale = _mod_rows(mod_ref, batch, row0, rows, shift_idx + 1)
    return (_rms(x) * nw) * (1.0 + scale) + shift


def _postnorm_residual(x, y, nw, mod_ref, batch, row0, gate_idx):
    gate = _mod_rows(mod_ref, batch, row0, x.shape[0], gate_idx)
    return x + gate * (_rms(y) * nw)


def _ada_kernel(c_ref, w_ref, b_ref, o_ref):
    c = c_ref[...]
    a = c * (1.0 / (1.0 + jnp.exp(-c)))
    a_hi, a_lo = _bf16_parts(a, 2)
    w_hi, w_lo = _bf16_parts(w_ref[0], 2)
    o_ref[0] = _dot(a_hi, w_hi) + _dot(a_lo, w_hi) + _dot(a_hi, w_lo) + b_ref[0]


def _ada_table(c_rows, ada_w, ada_b):
    depth, d, n = ada_w.shape
    tn = 1536
    return pl.pallas_call(
        _ada_kernel,
        grid=(depth, n // tn),
        in_specs=[
            pl.BlockSpec((MOD_ROWS, d), lambda l, j: (0, 0)),
            pl.BlockSpec((1, d, tn), lambda l, j: (l, 0, j)),
            pl.BlockSpec((1, 1, tn), lambda l, j: (l, 0, j)),
        ],
        out_specs=pl.BlockSpec((1, MOD_ROWS, tn), lambda l, j: (l, 0, j)),
        out_shape=jax.ShapeDtypeStruct((depth, MOD_ROWS, n), F32),
        compiler_params=pltpu.CompilerParams(
            dimension_semantics=("arbitrary", "arbitrary"), vmem_limit_bytes=VMEM_LIMIT),
        name="ada_table",
    )(c_rows, ada_w, ada_b.reshape(depth, 1, n))


def _in_proj_kernel(x_ref, mod_ref, nw_ref, w_ref, o_ref, h_ref):
    b, i, j = pl.program_id(0), pl.program_id(1), pl.program_id(2)

    @pl.when(j == 0)
    def _():
        h = _prenorm(x_ref[0], nw_ref[0:1, :], mod_ref, b, i * ROW_TILE, 0)
        h_ref[...] = h.astype(BF16)

    o_ref[0] = _dot(h_ref[...], w_ref[...])


def _in_proj(xs, mod, nw, w):
    bsz, s, d = xs.shape
    n = w.shape[1]
    tn = 1024
    return pl.pallas_call(
        _in_proj_kernel,
        grid=(bsz, s // ROW_TILE, n // tn),
        in_specs=[
            pl.BlockSpec((1, ROW_TILE, d), lambda b, i, j: (b, i, 0)),
            pl.BlockSpec(mod.shape, lambda b, i, j: (0, 0)),
            pl.BlockSpec(nw.shape, lambda b, i, j: (0, 0)),
            pl.BlockSpec((d, tn), lambda b, i, j: (0, j)),
        ],
        out_specs=pl.BlockSpec((1, ROW_TILE, tn), lambda b, i, j: (b, i, j)),
        out_shape=jax.ShapeDtypeStruct((bsz, s, n), F32),
        scratch_shapes=[pltpu.VMEM((ROW_TILE, d), BF16)],
        compiler_params=pltpu.CompilerParams(
            dimension_semantics=("arbitrary", "arbitrary", "arbitrary"),
            vmem_limit_bytes=VMEM_LIMIT),
        name="hgrn_in_proj",
    )(xs, mod, nw, w)


def _scan_levels():
    levels = []
    b = SCAN_CHUNK // 2
    while b >= SCAN_BASE:
        levels.append(b)
        b //= 2
    return levels


def _segment_rows(half):
    c = SCAN_CHUNK
    m = np.zeros((c, c), np.float32)
    for t in range(c):
        mid = (t // (2 * half)) * 2 * half + half
        if t >= mid:
            m[t, mid:t + 1] = 1.0
        else:
            m[t, t + 1:mid] = 1.0
    return m


def _scan_matrix(forward):
    c = SCAN_CHUNK
    t = np.arange(c)
    groups = [
        (t[None, :] <= t[:, None]).astype(np.float32),
        (t[None, :] > t[:, None]).astype(np.float32),
    ]
    groups += [_segment_rows(b) for b in _scan_levels()]
    groups.append(_segment_rows(SCAN_BASE // 2))
    if not forward:
        groups = [g[::-1, ::-1] for g in groups]
    return np.concatenate(groups, axis=0)


def _level_masks(forward):
    c = SCAN_CHUNK
    t = lax.broadcasted_iota(jnp.int32, (c, c), 0)
    s = lax.broadcasted_iota(jnp.int32, (c, c), 1)
    if not forward:
        t, s = (c - 1) - t, (c - 1) - s
    masks = []
    for b in _scan_levels():
        same = (t // (2 * b)) == (s // (2 * b))
        masks.append(same & (t % (2 * b) >= b) & (s % (2 * b) < b))
    same = (t // SCAN_BASE) == (s // SCAN_BASE)
    masks.append(same & (s <= t))
    return masks


def _scan_direction(q, v, z, m, lb, st_ref, o_ref, forward):
    c = SCAN_CHUNK
    n_levels = len(_scan_levels())
    sig = 1.0 / (1.0 + jnp.exp(-z))
    f = lb + (1.0 - lb) * sig
    log_f = jnp.log(jnp.maximum(f, F_FLOOR))
    k = (1.0 - lb) * (1.0 - sig)
    q = q * (HEAD_DIM ** -0.5)

    x = None
    for p in _bf16_parts(log_f, 3):
        xp = _dot(m, p)
        x = xp if x is None else x + xp
    grp = [x[i * c:(i + 1) * c] for i in range(n_levels + 3)]

    q_in = (q * jnp.exp(grp[0])).astype(BF16)
    k_st = (k * jnp.exp(grp[1])).astype(BF16)
    last = c - 1 if forward else 0
    e_last = jnp.exp(grp[0][last:last + 1])
    qd, kd = [], []
    for li in range(n_levels):
        e = jnp.exp(grp[2 + li])
        qd.append((q * e).astype(BF16))
        kd.append((k * e).astype(BF16))
    xd = grp[2 + n_levels]
    e_pos, e_neg = jnp.exp(xd), jnp.exp(-xd)
    row = lax.broadcasted_iota(jnp.int32, (c, 1), 0)
    if not forward:
        row = (c - 1) - row
    upper = (row % SCAN_BASE) >= (SCAN_BASE // 2)
    qd.append((q * jnp.where(upper, e_pos, e_neg)).astype(BF16))
    kd.append((k * jnp.where(upper, e_neg, e_pos)).astype(BF16))

    masks = _level_masks(forward)
    v16 = v.astype(BF16)
    contract_lanes = (((1,), (1,)), ((), ()))
    contract_rows = (((0,), (0,)), ((), ()))
    for h in range(HEADS):
        hs = slice(h * HEAD_DIM, (h + 1) * HEAD_DIM)
        a = jnp.zeros((c, c), F32)
        for li in range(n_levels + 1):
            a_l = lax.dot_general(qd[li][:, hs], kd[li][:, hs], contract_lanes,
                                  preferred_element_type=F32)
            a = jnp.where(masks[li], a_l, a)
        st = st_ref[h]
        o = _dot(a.astype(BF16), v16[:, hs])
        o = o + lax.dot_general(q_in[:, hs], st.astype(BF16), contract_lanes,
                                preferred_element_type=F32)
        o_ref[0, :, hs] = o
        st_ref[h] = st * e_last[:, hs] + lax.dot_general(
            v16[:, hs], k_st[:, hs], contract_rows, preferred_element_type=F32)


def _scan_kernel(lb_ref, mf_ref, mb_ref, qf_ref, vf_ref, zf_ref, qb_ref, vb_ref, zb_ref,
                 of_ref, ob_ref, sf_ref, sb_ref, *, layer):
    @pl.when(pl.program_id(1) == 0)
    def _():
        sf_ref[...] = jnp.zeros_like(sf_ref)
        sb_ref[...] = jnp.zeros_like(sb_ref)

    raw = lb_ref[...]
    ex = jnp.exp(raw - jnp.max(raw, axis=0, keepdims=True))
    soft = ex / jnp.sum(ex, axis=0, keepdims=True)
    lb = jnp.zeros((1, D_MODEL), F32)
    for i in range(1, layer + 1):
        lb = lb + soft[i:i + 1]

    _scan_direction(qf_ref[0], vf_ref[0], zf_ref[0], mf_ref[...], lb, sf_ref, of_ref, True)
    _scan_direction(qb_ref[0], vb_ref[0], zb_ref[0], mb_ref[...], lb, sb_ref, ob_ref, False)


def _scan(proj, hgrn_lb, layer):
    bsz, s, _ = proj.shape
    c = SCAN_CHUNK
    n_steps = s // c
    n_ctx = CTX_LEN // c
    mf = jnp.asarray(_scan_matrix(True), BF16)
    mb = jnp.asarray(_scan_matrix(False), BF16)

    def fwd_chunk(i):
        return i

    def bwd_chunk(i):
        return jnp.where(i < n_ctx, n_ctx - 1 - i, n_steps - 1 + n_ctx - i)

    def part(chunk_of, col):
        return pl.BlockSpec((1, c, D_MODEL), lambda b, i: (b, chunk_of(i), col))

    out_sds = jax.ShapeDtypeStruct((bsz, s, D_MODEL), F32)
    return pl.pallas_call(
        functools.partial(_scan_kernel, layer=layer),
        grid=(bsz, n_steps),
        in_specs=[
            pl.BlockSpec(hgrn_lb.shape, lambda b, i: (0, 0)),
            pl.BlockSpec(mf.shape, lambda b, i: (0, 0)),
            pl.BlockSpec(mb.shape, lambda b, i: (0, 0)),
            part(fwd_chunk, 0), part(fwd_chunk, 1), part(fwd_chunk, 2),
            part(bwd_chunk, 0), part(bwd_chunk, 1), part(bwd_chunk, 3),
        ],
        out_specs=[
            pl.BlockSpec((1, c, D_MODEL), lambda b, i: (b, fwd_chunk(i), 0)),
            pl.BlockSpec((1, c, D_MODEL), lambda b, i: (b, bwd_chunk(i), 0)),
        ],
        out_shape=[out_sds, out_sds],
        scratch_shapes=[pltpu.VMEM((HEADS, HEAD_DIM, HEAD_DIM), F32),
                        pltpu.VMEM((HEADS, HEAD_DIM, HEAD_DIM), F32)],
        compiler_params=pltpu.CompilerParams(
            dimension_semantics=("arbitrary", "arbitrary"), vmem_limit_bytes=VMEM_LIMIT),
        name="hgrn_scan",
    )(hgrn_lb, mf, mb, proj, proj, proj, proj, proj, proj)


def _readout_kernel(x_ref, of_ref, ob_ref, g_ref, mod_ref, nw_ref, gn_ref, w_ref, o_ref):
    b, i = pl.program_id(0), pl.program_id(1)
    o = of_ref[0] + ob_ref[0]
    g = g_ref[0]
    gate = g * (1.0 / (1.0 + jnp.exp(-g)))
    gn = gn_ref[...]
    heads = []
    for h in range(HEADS):
        hs = slice(h * HEAD_DIM, (h + 1) * HEAD_DIM)
        heads.append(_rms(o[:, hs]) * gn)
    y = jnp.concatenate(heads, axis=-1) * gate
    y = _dot(y.astype(BF16), w_ref[...])
    o_ref[0] = _postnorm_residual(x_ref[0], y, nw_ref[1:2, :], mod_ref, b, i * ROW_TILE, 2)


def _readout(xs, o_f, o_b, proj, mod, nw, gnorm, w):
    bsz, s, d = xs.shape
    tile = pl.BlockSpec((1, ROW_TILE, d), lambda b, i: (b, i, 0))
    return pl.pallas_call(
        _readout_kernel,
        grid=(bsz, s // ROW_TILE),
        in_specs=[
            tile, tile, tile,
            pl.BlockSpec((1, ROW_TILE, d), lambda b, i: (b, i, 4)),
            pl.BlockSpec(mod.shape, lambda b, i: (0, 0)),
            pl.BlockSpec(nw.shape, lambda b, i: (0, 0)),
            pl.BlockSpec(gnorm.shape, lambda b, i: (0, 0)),
            pl.BlockSpec(w.shape, lambda b, i: (0, 0)),
        ],
        out_specs=tile,
        out_shape=jax.ShapeDtypeStruct(xs.shape, F32),
        compiler_params=pltpu.CompilerParams(
            dimension_semantics=("arbitrary", "arbitrary"), vmem_limit_bytes=VMEM_LIMIT),
        name="hgrn_readout",
    )(xs, o_f, o_b, proj, mod, nw, gnorm, w)


def _conv_kernel(x_ref, mod_ref, nw_ref, wb_ref, wc_ref, wx_ref, cw_ref, wo_ref, o_ref):
    b, i = pl.program_id(0), pl.program_id(1)
    row0 = i * ROW_TILE
    x = x_ref[0]
    h = _prenorm(x, nw_ref[0:1, :], mod_ref, b, row0, 0).astype(BF16)
    u = _dot(h, wc_ref[...]) * _dot(h, wx_ref[...])
    pos = row0 + lax.broadcasted_iota(jnp.int32, (ROW_TILE, 1), 0)
    in_ctx = pos < CTX_LEN
    has_prev = (pos != 0) & (in_ctx | (pos % GRID_W != 0))
    has_next = (pos != CTX_LEN - 1) & (in_ctx | (pos % GRID_W != GRID_W - 1))
    u_prev = jnp.where(has_prev, pltpu.roll(u, 1, axis=0), 0.0)
    u_next = jnp.where(has_next, pltpu.roll(u, ROW_TILE - 1, axis=0), 0.0)
    y = u_prev * cw_ref[0:1, :] + u * cw_ref[1:2, :] + u_next * cw_ref[2:3, :]
    y = _dot(h, wb_ref[...]) * y
    y = _dot(y.astype(BF16), wo_ref[...])
    o_ref[0] = _postnorm_residual(x, y, nw_ref[1:2, :], mod_ref, b, row0, 2)


def _conv_mixer(xs, mod, nw, w_in, conv_w, w_out):
    bsz, s, d = xs.shape
    tile = pl.BlockSpec((1, ROW_TILE, d), lambda b, i: (b, i, 0))

    def full(a):
        return pl.BlockSpec(a.shape, lambda b, i: (0, 0))

    return pl.pallas_call(
        _conv_kernel,
        grid=(bsz, s // ROW_TILE),
        in_specs=[
            tile, full(mod), full(nw),
            pl.BlockSpec((d, d), lambda b, i: (0, 0)),
            pl.BlockSpec((d, d), lambda b, i: (0, 1)),
            pl.BlockSpec((d, d), lambda b, i: (0, 2)),
            full(conv_w), full(w_out),
        ],
        out_specs=tile,
        out_shape=jax.ShapeDtypeStruct(xs.shape, F32),
        compiler_params=pltpu.CompilerParams(
            dimension_semantics=("arbitrary", "arbitrary"), vmem_limit_bytes=VMEM_LIMIT),
        name="conv_mixer",
    )(xs, mod, nw, w_in, w_in, w_in, conv_w, w_out)


def _ffn_kernel(x_ref, mod_ref, nw_ref, wg_ref, wu_ref, wo_ref, o_ref, h_ref, acc_ref):
    b, i, j = pl.program_id(0), pl.program_id(1), pl.program_id(2)
    row0 = i * ROW_TILE

    @pl.when(j == 0)
    def _():
        h = _prenorm(x_ref[0], nw_ref[2:3, :], mod_ref, b, row0, 3)
        h_ref[...] = h.astype(BF16)
        acc_ref[...] = jnp.zeros_like(acc_ref)

    h = h_ref[...]
    gate = _dot(h, wg_ref[...])
    up = _dot(h, wu_ref[...])
    act = (gate * (1.0 / (1.0 + jnp.exp(-gate)))) * up
    acc_ref[...] += _dot(act.astype(BF16), wo_ref[...])

    @pl.when(j == pl.num_programs(2) - 1)
    def _():
        o_ref[0] = _postnorm_residual(x_ref[0], acc_ref[...], nw_ref[3:4, :], mod_ref, b, row0, 5)


def _ffn(xs, mod, nw, w_in, w_out):
    bsz, s, d = xs.shape
    hidden = w_out.shape[0]
    tf = hidden // 2
    n_f = hidden // tf
    tile = pl.BlockSpec((1, ROW_TILE, d), lambda b, i, j: (b, i, 0))
    return pl.pallas_call(
        _ffn_kernel,
        grid=(bsz, s // ROW_TILE, n_f),
        in_specs=[
            tile,
            pl.BlockSpec(mod.shape, lambda b, i, j: (0, 0)),
            pl.BlockSpec(nw.shape, lambda b, i, j: (0, 0)),
            pl.BlockSpec((d, tf), lambda b, i, j: (0, j)),
            pl.BlockSpec((d, tf), lambda b, i, j: (0, j + n_f)),
            pl.BlockSpec((tf, d), lambda b, i, j: (j, 0)),
        ],
        out_specs=tile,
        out_shape=jax.ShapeDtypeStruct(xs.shape, F32),
        scratch_shapes=[pltpu.VMEM((ROW_TILE, d), BF16), pltpu.VMEM((ROW_TILE, d), F32)],
        compiler_params=pltpu.CompilerParams(
            dimension_semantics=("arbitrary", "arbitrary", "arbitrary"),
            vmem_limit_bytes=VMEM_LIMIT),
        name="swiglu_ffn",
    )(xs, mod, nw, w_in, w_in, w_out)


def kernel(x, c, ctx, c_ctx, ada_w, ada_b, norm_w, hgrn_w_in, hgrn_w_out, hgrn_gnorm, hgrn_lb,
           conv_w_in, conv_w, conv_w_out, ffn_w_in, ffn_w_out):
    bsz, seq, d = x.shape
    depth = ada_w.shape[0]
    assert d == D_MODEL and ctx.shape[1] == CTX_LEN and bsz < MOD_ROWS
    assert (seq + CTX_LEN) % ROW_TILE == 0 and ROW_TILE % CTX_LEN == 0 and seq % GRID_W == 0

    c_rows = jnp.zeros((MOD_ROWS, d), F32).at[:bsz].set(c).at[MOD_ROWS - 1].set(c_ctx)
    mod_all = _ada_table(c_rows, ada_w, ada_b)

    xs = jnp.concatenate([ctx, x], axis=1)
    for l in range(depth):
        j = l // 2
        mod, nw = mod_all[l], norm_w[l]
        if l % 2 == 0:
            proj = _in_proj(xs, mod, nw, hgrn_w_in[j].astype(BF16))
            o_f, o_b = _scan(proj, hgrn_lb, j)
            gnorm = hgrn_gnorm[j].reshape(1, HEAD_DIM)
            xs = _readout(xs, o_f, o_b, proj, mod, nw, gnorm, hgrn_w_out[j].astype(BF16))
        else:
            xs = _conv_mixer(xs, mod, nw, conv_w_in[j].astype(BF16), conv_w[j],
                             conv_w_out[j].astype(BF16))
        xs = _ffn(xs, mod, nw, ffn_w_in[l].astype(BF16), ffn_w_out[l].astype(BF16))
    return xs[:, CTX_LEN:, :]
```

```python
import functools

import numpy as np
import jax
import jax.numpy as jnp
from jax import lax
from jax.experimental import pallas as pl
from jax.experimental.pallas import tpu as pltpu

D_MODEL = 1024
CTX_LEN = 256
GRID_W = 64
HEADS = 8
HEAD_DIM = D_MODEL // HEADS
CONV_WIDTH = 3
N_MOD = 6
EPS = 1e-6
F_FLOOR = 1e-6

SCAN_BLOCK = 256
SCAN_CHUNK = 128
SCAN_BASE = 8
ROW_TILE = 768
MOD_ROWS = 8
VMEM_LIMIT = 56 * 1024 * 1024

F32 = jnp.float32
BF16 = jnp.bfloat16


def _bf16_parts(x, n):
    parts = []
    r = x
    for i in range(n):
        p = r.astype(BF16)
        parts.append(p)
        if i + 1 < n:
            r = r - p.astype(F32)
    return parts


def _dot(a, b):
    return jnp.dot(a, b, preferred_element_type=F32)


def _rms(x):
    return x * lax.rsqrt(jnp.mean(x * x, axis=-1, keepdims=True) + EPS)


def _mod_rows(mod_ref, batch, row0, rows, idx):
    cols = slice(idx * D_MODEL, (idx + 1) * D_MODEL)
    m_b = mod_ref[pl.ds(batch, 1), cols]
    m_c = mod_ref[pl.ds(MOD_ROWS - 1, 1), cols]
    pos = row0 + lax.broadcasted_iota(jnp.int32, (rows, 1), 0)
    return jnp.where(pos < CTX_LEN, m_c, m_b)


def _prenorm(x, nw, mod_ref, batch, row0, shift_idx):
    rows = x.shape[0]
    shift = _mod_rows(mod_ref, batch, row0, rows, shift_idx)
    scale = _mod_rows(mod_ref, batch, row0, rows, shift_idx + 1)
    return (_rms(x) * nw) * (1.0 + scale) + shift


def _postnorm_residual(x, y, nw, mod_ref, batch, row0, gate_idx):
    gate = _mod_rows(mod_ref, batch, row0, x.shape[0], gate_idx)
    return x + gate * (_rms(y) * nw)


def _ada_kernel(c_ref, w_ref, b_ref, o_ref):
    c = c_ref[...]
    a = c * (1.0 / (1.0 + jnp.exp(-c)))
    a_hi, a_lo = _bf16_parts(a, 2)
    w_hi, w_lo = _bf16_parts(w_ref[0], 2)
    o_ref[0] = _dot(a_hi, w_hi) + _dot(a_lo, w_hi) + _dot(a_hi, w_lo) + b_ref[0]


def _ada_table(c_rows, ada_w, ada_b):
    depth, d, n = ada_w.shape
    tn = 1536
    return pl.pallas_call(
        _ada_kernel,
        grid=(depth, n // tn),
        in_specs=[
            pl.BlockSpec((MOD_ROWS, d), lambda l, j: (0, 0)),
            pl.BlockSpec((1, d, tn), lambda l, j: (l, 0, j)),
            pl.BlockSpec((1, 1, tn), lambda l, j: (l, 0, j)),
        ],
        out_specs=pl.BlockSpec((1, MOD_ROWS, tn), lambda l, j: (l, 0, j)),
        out_shape=jax.ShapeDtypeStruct((depth, MOD_ROWS, n), F32),
        compiler_params=pltpu.CompilerParams(
            dimension_semantics=("arbitrary", "arbitrary"), vmem_limit_bytes=VMEM_LIMIT),
        name="ada_table",
    )(c_rows, ada_w, ada_b.reshape(depth, 1, n))


def _in_proj_kernel(x_ref, mod_ref, nw_ref, w_ref, o_ref, h_ref):
    b, i, j = pl.program_id(0), pl.program_id(1), pl.program_id(2)

    @pl.when(j == 0)
    def _():
        h = _prenorm(x_ref[0], nw_ref[0:1, :], mod_ref, b, i * ROW_TILE, 0)
        h_ref[...] = h.astype(BF16)

    o_ref[0] = _dot(h_ref[...], w_ref[...])


def _in_proj(xs, mod, nw, w):
    bsz, s, d = xs.shape
    n = w.shape[1]
    tn = 1024
    return pl.pallas_call(
        _in_proj_kernel,
        grid=(bsz, s // ROW_TILE, n // tn),
        in_specs=[
            pl.BlockSpec((1, ROW_TILE, d), lambda b, i, j: (b, i, 0)),
            pl.BlockSpec(mod.shape, lambda b, i, j: (0, 0)),
            pl.BlockSpec(nw.shape, lambda b, i, j: (0, 0)),
            pl.BlockSpec((d, tn), lambda b, i, j: (0, j)),
        ],
        out_specs=pl.BlockSpec((1, ROW_TILE, tn), lambda b, i, j: (b, i, j)),
        out_shape=jax.ShapeDtypeStruct((bsz, s, n), F32),
        scratch_shapes=[pltpu.VMEM((ROW_TILE, d), BF16)],
        compiler_params=pltpu.CompilerParams(
            dimension_semantics=("arbitrary", "arbitrary", "arbitrary"),
            vmem_limit_bytes=VMEM_LIMIT),
        name="hgrn_in_proj",
    )(xs, mod, nw, w)


def _scan_levels():
    levels = []
    b = SCAN_CHUNK // 2
    while b >= SCAN_BASE:
        levels.append(b)
        b //= 2
    return levels


def _cumsum_matrix(forward):
    t = np.arange(SCAN_CHUNK)
    tri = t[None, :] <= t[:, None] if forward else t[None, :] >= t[:, None]
    return tri.astype(np.float32)


def _level_masks(forward):
    c = SCAN_CHUNK
    t = lax.broadcasted_iota(jnp.int32, (c, c), 0)
    s = lax.broadcasted_iota(jnp.int32, (c, c), 1)
    if not forward:
        t, s = (c - 1) - t, (c - 1) - s
    masks = []
    for b in _scan_levels():
        same = (t // (2 * b)) == (s // (2 * b))
        masks.append(same & (t % (2 * b) >= b) & (s % (2 * b) < b))
    same = (t // SCAN_BASE) == (s // SCAN_BASE)
    masks.append(same & (s <= t))
    return masks


def _minus_block_ref(c, half, forward):
    if 2 * half == SCAN_BASE:
        c3 = c.reshape(SCAN_CHUNK // SCAN_BASE, SCAN_BASE, c.shape[-1])
        r = half - 1 if forward else half
        return (c3 - c3[:, r:r + 1, :]).reshape(c.shape)
    pieces = []
    for start in range(0, SCAN_CHUNK, 2 * half):
        r = start + half - 1 if forward else start + half
        pieces.append(c[start:start + 2 * half] - c[r:r + 1])
    return pieces[0] if len(pieces) == 1 else jnp.concatenate(pieces, axis=0)


def _scan_chunk(q, k, v16, g, tri, masks, st_ref, o_ref, rows, forward):
    c = None
    for p in _bf16_parts(g, 3):
        cp = _dot(tri, p)
        c = cp if c is None else c + cp
    last = SCAN_CHUNK - 1 if forward else 0
    c_last = c[last:last + 1]
    q_in = (q * jnp.exp2(c)).astype(BF16)
    k_st = (k * jnp.exp2(c_last - c)).astype(BF16)
    e_last = jnp.exp2(c_last)
    qd, kd = [], []
    for b in _scan_levels():
        e = jnp.exp2(-jnp.abs(_minus_block_ref(c, b, forward)))
        qd.append((q * e).astype(BF16))
        kd.append((k * e).astype(BF16))
    xd = _minus_block_ref(c, SCAN_BASE // 2, forward)
    qd.append((q * jnp.exp2(xd)).astype(BF16))
    kd.append((k * jnp.exp2(-xd)).astype(BF16))

    contract_lanes = (((1,), (1,)), ((), ()))
    contract_rows = (((0,), (0,)), ((), ()))
    for h in range(HEADS):
        hs = slice(h * HEAD_DIM, (h + 1) * HEAD_DIM)
        a = jnp.zeros((SCAN_CHUNK, SCAN_CHUNK), F32)
        for li in range(len(masks)):
            a_l = lax.dot_general(qd[li][:, hs], kd[li][:, hs], contract_lanes,
                                  preferred_element_type=F32)
            a = jnp.where(masks[li], a_l, a)
        st = st_ref[h]
        o = _dot(a.astype(BF16), v16[:, hs])
        o = o + lax.dot_general(q_in[:, hs], st.astype(BF16), contract_lanes,
                                preferred_element_type=F32)
        o_ref[0, rows, hs] = o
        st_ref[h] = st * e_last[:, hs] + lax.dot_general(
            v16[:, hs], k_st[:, hs], contract_rows, preferred_element_type=F32)


def _scan_direction(q, v, z, tri, lb, st_ref, o_ref, forward):
    rest = 1.0 - lb
    sig = 1.0 / (1.0 + jnp.exp(-z))
    a = rest * sig
    g = jnp.log(jnp.maximum(lb + a, F_FLOOR)) * (1.0 / np.log(2.0))
    k = rest - a
    q = q * (HEAD_DIM ** -0.5)
    v16 = v.astype(BF16)
    masks = _level_masks(forward)
    n_chunks = SCAN_BLOCK // SCAN_CHUNK
    for ci in (range(n_chunks) if forward else reversed(range(n_chunks))):
        rows = slice(ci * SCAN_CHUNK, (ci + 1) * SCAN_CHUNK)
        _scan_chunk(q[rows], k[rows], v16[rows], g[rows], tri, masks, st_ref, o_ref, rows, forward)


def _scan_kernel(lb_ref, mf_ref, mb_ref, qf_ref, vf_ref, zf_ref, qb_ref, vb_ref, zb_ref,
                 of_ref, ob_ref, sf_ref, sb_ref, *, layer):
    @pl.when(pl.program_id(1) == 0)
    def _():
        sf_ref[...] = jnp.zeros_like(sf_ref)
        sb_ref[...] = jnp.zeros_like(sb_ref)

    raw = lb_ref[...]
    ex = jnp.exp(raw - jnp.max(raw, axis=0, keepdims=True))
    soft = ex / jnp.sum(ex, axis=0, keepdims=True)
    lb = jnp.zeros((1, D_MODEL), F32)
    for i in range(1, layer + 1):
        lb = lb + soft[i:i + 1]

    _scan_direction(qf_ref[0], vf_ref[0], zf_ref[0], mf_ref[...], lb, sf_ref, of_ref, True)
    _scan_direction(qb_ref[0], vb_ref[0], zb_ref[0], mb_ref[...], lb, sb_ref, ob_ref, False)


def _scan(proj, hgrn_lb, layer):
    bsz, s, _ = proj.shape
    c = SCAN_BLOCK
    n_steps = s // c
    n_ctx = CTX_LEN // c
    mf = jnp.asarray(_cumsum_matrix(True), BF16)
    mb = jnp.asarray(_cumsum_matrix(False), BF16)

    def fwd_chunk(i):
        return i

    def bwd_chunk(i):
        return jnp.where(i < n_ctx, n_ctx - 1 - i, n_steps - 1 + n_ctx - i)

    def part(chunk_of, col):
        return pl.BlockSpec((1, c, D_MODEL), lambda b, i: (b, chunk_of(i), col))

    out_sds = jax.ShapeDtypeStruct((bsz, s, D_MODEL), F32)
    return pl.pallas_call(
        functools.partial(_scan_kernel, layer=layer),
        grid=(bsz, n_steps),
        in_specs=[
            pl.BlockSpec(hgrn_lb.shape, lambda b, i: (0, 0)),
            pl.BlockSpec(mf.shape, lambda b, i: (0, 0)),
            pl.BlockSpec(mb.shape, lambda b, i: (0, 0)),
            part(fwd_chunk, 0), part(fwd_chunk, 1), part(fwd_chunk, 2),
            part(bwd_chunk, 0), part(bwd_chunk, 1), part(bwd_chunk, 3),
        ],
        out_specs=[
            pl.BlockSpec((1, c, D_MODEL), lambda b, i: (b, fwd_chunk(i), 0)),
            pl.BlockSpec((1, c, D_MODEL), lambda b, i: (b, bwd_chunk(i), 0)),
        ],
        out_shape=[out_sds, out_sds],
        scratch_shapes=[pltpu.VMEM((HEADS, HEAD_DIM, HEAD_DIM), F32),
                        pltpu.VMEM((HEADS, HEAD_DIM, HEAD_DIM), F32)],
        compiler_params=pltpu.CompilerParams(
            dimension_semantics=("arbitrary", "arbitrary"), vmem_limit_bytes=VMEM_LIMIT),
        name="hgrn_scan",
    )(hgrn_lb, mf, mb, proj, proj, proj, proj, proj, proj)


def _readout_kernel(x_ref, of_ref, ob_ref, g_ref, mod_ref, nw_ref, gn_ref, w_ref, o_ref):
    b, i = pl.program_id(0), pl.program_id(1)
    o = of_ref[0] + ob_ref[0]
    g = g_ref[0]
    gate = g * (1.0 / (1.0 + jnp.exp(-g)))
    gn = gn_ref[...]
    heads = []
    for h in range(HEADS):
        hs = slice(h * HEAD_DIM, (h + 1) * HEAD_DIM)
        heads.append(_rms(o[:, hs]) * gn)
    y = jnp.concatenate(heads, axis=-1) * gate
    y = _dot(y.astype(BF16), w_ref[...])
    o_ref[0] = _postnorm_residual(x_ref[0], y, nw_ref[1:2, :], mod_ref, b, i * ROW_TILE, 2)


def _readout(xs, o_f, o_b, proj, mod, nw, gnorm, w):
    bsz, s, d = xs.shape
    tile = pl.BlockSpec((1, ROW_TILE, d), lambda b, i: (b, i, 0))
    return pl.pallas_call(
        _readout_kernel,
        grid=(bsz, s // ROW_TILE),
        in_specs=[
            tile, tile, tile,
            pl.BlockSpec((1, ROW_TILE, d), lambda b, i: (b, i, 4)),
            pl.BlockSpec(mod.shape, lambda b, i: (0, 0)),
            pl.BlockSpec(nw.shape, lambda b, i: (0, 0)),
            pl.BlockSpec(gnorm.shape, lambda b, i: (0, 0)),
            pl.BlockSpec(w.shape, lambda b, i: (0, 0)),
        ],
        out_specs=tile,
        out_shape=jax.ShapeDtypeStruct(xs.shape, F32),
        compiler_params=pltpu.CompilerParams(
            dimension_semantics=("arbitrary", "arbitrary"), vmem_limit_bytes=VMEM_LIMIT),
        name="hgrn_readout",
    )(xs, o_f, o_b, proj, mod, nw, gnorm, w)


def _conv_kernel(x_ref, mod_ref, nw_ref, wb_ref, wc_ref, wx_ref, cw_ref, wo_ref, o_ref):
    b, i = pl.program_id(0), pl.program_id(1)
    row0 = i * ROW_TILE
    x = x_ref[0]
    h = _prenorm(x, nw_ref[0:1, :], mod_ref, b, row0, 0).astype(BF16)
    u = _dot(h, wc_ref[...]) * _dot(h, wx_ref[...])
    pos = row0 + lax.broadcasted_iota(jnp.int32, (ROW_TILE, 1), 0)
    in_ctx = pos < CTX_LEN
    has_prev = (pos != 0) & (in_ctx | (pos % GRID_W != 0))
    has_next = (pos != CTX_LEN - 1) & (in_ctx | (pos % GRID_W != GRID_W - 1))
    u_prev = jnp.where(has_prev, pltpu.roll(u, 1, axis=0), 0.0)
    u_next = jnp.where(has_next, pltpu.roll(u, ROW_TILE - 1, axis=0), 0.0)
    y = u_prev * cw_ref[0:1, :] + u * cw_ref[1:2, :] + u_next * cw_ref[2:3, :]
    y = _dot(h, wb_ref[...]) * y
    y = _dot(y.astype(BF16), wo_ref[...])
    o_ref[0] = _postnorm_residual(x, y, nw_ref[1:2, :], mod_ref, b, row0, 2)


def _conv_mixer(xs, mod, nw, w_in, conv_w, w_out):
    bsz, s, d = xs.shape
    tile = pl.BlockSpec((1, ROW_TILE, d), lambda b, i: (b, i, 0))

    def full(a):
        return pl.BlockSpec(a.shape, lambda b, i: (0, 0))

    return pl.pallas_call(
        _conv_kernel,
        grid=(bsz, s // ROW_TILE),
        in_specs=[
            tile, full(mod), full(nw),
            pl.BlockSpec((d, d), lambda b, i: (0, 0)),
            pl.BlockSpec((d, d), lambda b, i: (0, 1)),
            pl.BlockSpec((d, d), lambda b, i: (0, 2)),
            full(conv_w), full(w_out),
        ],
        out_specs=tile,
        out_shape=jax.ShapeDtypeStruct(xs.shape, F32),
        compiler_params=pltpu.CompilerParams(
            dimension_semantics=("arbitrary", "arbitrary"), vmem_limit_bytes=VMEM_LIMIT),
        name="conv_mixer",
    )(xs, mod, nw, w_in, w_in, w_in, conv_w, w_out)


def _ffn_kernel(x_ref, mod_ref, nw_ref, wg_ref, wu_ref, wo_ref, o_ref, h_ref, acc_ref):
    b, i, j = pl.program_id(0), pl.program_id(1), pl.program_id(2)
    row0 = i * ROW_TILE

    @pl.when(j == 0)
    def _():
        h = _prenorm(x_ref[0], nw_ref[2:3, :], mod_ref, b, row0, 3)
        h_ref[...] = h.astype(BF16)
        acc_ref[...] = jnp.zeros_like(acc_ref)

    h = h_ref[...]
    gate = _dot(h, wg_ref[...])
    up = _dot(h, wu_ref[...])
    act = (gate * (1.0 / (1.0 + jnp.exp(-gate)))) * up
    acc_ref[...] += _dot(act.astype(BF16), wo_ref[...])

    @pl.when(j == pl.num_programs(2) - 1)
    def _():
        o_ref[0] = _postnorm_residual(x_ref[0], acc_ref[...], nw_ref[3:4, :], mod_ref, b, row0, 5)


def _ffn(xs, mod, nw, w_in, w_out):
    bsz, s, d = xs.shape
    hidden = w_out.shape[0]
    tf = hidden // 2
    n_f = hidden // tf
    tile = pl.BlockSpec((1, ROW_TILE, d), lambda b, i, j: (b, i, 0))
    return pl.pallas_call(
        _ffn_kernel,
        grid=(bsz, s // ROW_TILE, n_f),
        in_specs=[
            tile,
            pl.BlockSpec(mod.shape, lambda b, i, j: (0, 0)),
            pl.BlockSpec(nw.shape, lambda b, i, j: (0, 0)),
            pl.BlockSpec((d, tf), lambda b, i, j: (0, j)),
            pl.BlockSpec((d, tf), lambda b, i, j: (0, j + n_f)),
            pl.BlockSpec((tf, d), lambda b, i, j: (j, 0)),
        ],
        out_specs=tile,
        out_shape=jax.ShapeDtypeStruct(xs.shape, F32),
        scratch_shapes=[pltpu.VMEM((ROW_TILE, d), BF16), pltpu.VMEM((ROW_TILE, d), F32)],
        compiler_params=pltpu.CompilerParams(
            dimension_semantics=("arbitrary", "arbitrary", "arbitrary"),
            vmem_limit_bytes=VMEM_LIMIT),
        name="swiglu_ffn",
    )(xs, mod, nw, w_in, w_in, w_out)


def kernel(x, c, ctx, c_ctx, ada_w, ada_b, norm_w, hgrn_w_in, hgrn_w_out, hgrn_gnorm, hgrn_lb,
           conv_w_in, conv_w, conv_w_out, ffn_w_in, ffn_w_out):
    bsz, seq, d = x.shape
    depth = ada_w.shape[0]
    assert d == D_MODEL and ctx.shape[1] == CTX_LEN and bsz < MOD_ROWS
    assert (seq + CTX_LEN) % ROW_TILE == 0 and ROW_TILE % CTX_LEN == 0 and seq % GRID_W == 0

    c_rows = jnp.zeros((MOD_ROWS, d), F32).at[:bsz].set(c).at[MOD_ROWS - 1].set(c_ctx)
    mod_all = _ada_table(c_rows, ada_w, ada_b)

    xs = jnp.concatenate([ctx, x], axis=1)
    for l in range(depth):
        j = l // 2
        mod, nw = mod_all[l], norm_w[l]
        if l % 2 == 0:
            proj = _in_proj(xs, mod, nw, hgrn_w_in[j].astype(BF16))
            o_f, o_b = _scan(proj, hgrn_lb, j)
            gnorm = hgrn_gnorm[j].reshape(1, HEAD_DIM)
            xs = _readout(xs, o_f, o_b, proj, mod, nw, gnorm, hgrn_w_out[j].astype(BF16))
        else:
            xs = _conv_mixer(xs, mod, nw, conv_w_in[j].astype(BF16), conv_w[j],
                             conv_w_out[j].astype(BF16))
        xs = _ffn(xs, mod, nw, ffn_w_in[l].astype(BF16), ffn_w_out[l].astype(BF16))
    return xs[:, CTX_LEN:, :]
```

```python
import functools

import numpy as np
import jax
import jax.numpy as jnp
from jax import lax
from jax.experimental import pallas as pl
from jax.experimental.pallas import tpu as pltpu

D_MODEL = 1024
CTX_LEN = 256
GRID_W = 64
HEADS = 8
HEAD_DIM = D_MODEL // HEADS
CONV_WIDTH = 3
N_MOD = 6
EPS = 1e-6
F_FLOOR = 1e-6

SCAN_BLOCK = 256
SCAN_CHUNK = 128
SCAN_BASE = 8
SCAN_SKEW = (0, 1, 2, 3)
CUMSUM_PARTS = 2
ROW_TILE = 768
MOD_ROWS = 8
VMEM_LIMIT = 56 * 1024 * 1024

F32 = jnp.float32
BF16 = jnp.bfloat16


def _bf16_parts(x, n):
    parts = []
    r = x
    for i in range(n):
        p = r.astype(BF16)
        parts.append(p)
        if i + 1 < n:
            r = r - p.astype(F32)
    return parts


def _dot(a, b):
    return jnp.dot(a, b, preferred_element_type=F32)


def _rms(x):
    return x * lax.rsqrt(jnp.mean(x * x, axis=-1, keepdims=True) + EPS)


def _mod_rows(mod_ref, batch, row0, rows, idx):
    cols = slice(idx * D_MODEL, (idx + 1) * D_MODEL)
    m_b = mod_ref[pl.ds(batch, 1), cols]
    m_c = mod_ref[pl.ds(MOD_ROWS - 1, 1), cols]
    pos = row0 + lax.broadcasted_iota(jnp.int32, (rows, 1), 0)
    return jnp.where(pos < CTX_LEN, m_c, m_b)


def _prenorm(x, nw, mod_ref, batch, row0, shift_idx):
    rows = x.shape[0]
    shift = _mod_rows(mod_ref, batch, row0, rows, shift_idx)
    scale = _mod_rows(mod_ref, batch, row0, rows, shift_idx + 1)
    return (_rms(x) * nw) * (1.0 + scale) + shift


def _postnorm_residual(x, y, nw, mod_ref, batch, row0, gate_idx):
    gate = _mod_rows(mod_ref, batch, row0, x.shape[0], gate_idx)
    return x + gate * (_rms(y) * nw)


def _ada_kernel(c_ref, w_ref, b_ref, o_ref):
    c = c_ref[...]
    a = c * (1.0 / (1.0 + jnp.exp(-c)))
    a_hi, a_lo = _bf16_parts(a, 2)
    w_hi, w_lo = _bf16_parts(w_ref[0], 2)
    o_ref[0] = _dot(a_hi, w_hi) + _dot(a_lo, w_hi) + _dot(a_hi, w_lo) + b_ref[0]


def _ada_table(c_rows, ada_w, ada_b):
    depth, d, n = ada_w.shape
    tn = 1536
    return pl.pallas_call(
        _ada_kernel,
        grid=(depth, n // tn),
        in_specs=[
            pl.BlockSpec((MOD_ROWS, d), lambda l, j: (0, 0)),
            pl.BlockSpec((1, d, tn), lambda l, j: (l, 0, j)),
            pl.BlockSpec((1, 1, tn), lambda l, j: (l, 0, j)),
        ],
        out_specs=pl.BlockSpec((1, MOD_ROWS, tn), lambda l, j: (l, 0, j)),
        out_shape=jax.ShapeDtypeStruct((depth, MOD_ROWS, n), F32),
        compiler_params=pltpu.CompilerParams(
            dimension_semantics=("arbitrary", "arbitrary"), vmem_limit_bytes=VMEM_LIMIT),
        name="ada_table",
    )(c_rows, ada_w, ada_b.reshape(depth, 1, n))


def _in_proj_kernel(x_ref, mod_ref, nw_ref, w_ref, o_ref, h_ref):
    b, i, j = pl.program_id(0), pl.program_id(1), pl.program_id(2)

    @pl.when(j == 0)
    def _():
        h = _prenorm(x_ref[0], nw_ref[0:1, :], mod_ref, b, i * ROW_TILE, 0)
        h_ref[...] = h.astype(BF16)

    o_ref[0] = _dot(h_ref[...], w_ref[...])


def _in_proj(xs, mod, nw, w):
    bsz, s, d = xs.shape
    n = w.shape[1]
    tn = 1024
    return pl.pallas_call(
        _in_proj_kernel,
        grid=(bsz, s // ROW_TILE, n // tn),
        in_specs=[
            pl.BlockSpec((1, ROW_TILE, d), lambda b, i, j: (b, i, 0)),
            pl.BlockSpec(mod.shape, lambda b, i, j: (0, 0)),
            pl.BlockSpec(nw.shape, lambda b, i, j: (0, 0)),
            pl.BlockSpec((d, tn), lambda b, i, j: (0, j)),
        ],
        out_specs=pl.BlockSpec((1, ROW_TILE, tn), lambda b, i, j: (b, i, j)),
        out_shape=jax.ShapeDtypeStruct((bsz, s, n), F32),
        scratch_shapes=[pltpu.VMEM((ROW_TILE, d), BF16)],
        compiler_params=pltpu.CompilerParams(
            dimension_semantics=("arbitrary", "arbitrary", "arbitrary"),
            vmem_limit_bytes=VMEM_LIMIT),
        name="hgrn_in_proj",
    )(xs, mod, nw, w)


def _scan_levels():
    levels = []
    b = SCAN_CHUNK // 2
    while b >= SCAN_BASE:
        levels.append(b)
        b //= 2
    return levels


def _cumsum_matrix(forward):
    t = np.arange(SCAN_CHUNK)
    tri = t[None, :] <= t[:, None] if forward else t[None, :] >= t[:, None]
    return tri.astype(np.float32)


def _level_masks(forward):
    c = SCAN_CHUNK
    t = lax.broadcasted_iota(jnp.int32, (c, c), 0)
    s = lax.broadcasted_iota(jnp.int32, (c, c), 1)
    if not forward:
        t, s = (c - 1) - t, (c - 1) - s
    masks = []
    for b in _scan_levels():
        same = (t // (2 * b)) == (s // (2 * b))
        masks.append(same & (t % (2 * b) >= b) & (s % (2 * b) < b))
    same = (t // SCAN_BASE) == (s // SCAN_BASE)
    masks.append(same & (s <= t))
    return masks


def _pair_level(q, k, c, half, forward):
    packed_rows = 16
    qs, ks = [], []
    for start in range(0, SCAN_CHUNK, 2 * half):
        lo, hi = slice(start, start + half), slice(start + half, start + 2 * half)
        if forward:
            ref, q_rows, k_rows = c[start + half - 1:start + half], hi, lo
        else:
            ref, q_rows, k_rows = c[start + half:start + half + 1], lo, hi
        eq = jnp.exp2(c[q_rows] - ref)
        ek = jnp.exp2(ref - c[k_rows])
        if half % packed_rows == 0:
            zero = jnp.zeros((half, c.shape[-1]), BF16)
            qe, ke = (q[q_rows] * eq).astype(BF16), (k[k_rows] * ek).astype(BF16)
            qs += [zero, qe] if forward else [qe, zero]
            ks += [ke, zero] if forward else [zero, ke]
        else:
            e = jnp.concatenate([ek, eq] if forward else [eq, ek], axis=0)
            rows = slice(start, start + 2 * half)
            qs.append((q[rows] * e).astype(BF16))
            ks.append((k[rows] * e).astype(BF16))
    return jnp.concatenate(qs, axis=0), jnp.concatenate(ks, axis=0)


def _diag_level(q, k, c, forward):
    c3 = c.reshape(SCAN_CHUNK // SCAN_BASE, SCAN_BASE, c.shape[-1])
    r = SCAN_BASE // 2 - 1 if forward else SCAN_BASE // 2
    x = (c3 - c3[:, r:r + 1, :]).reshape(c.shape)
    return (q * jnp.exp2(x)).astype(BF16), (k * jnp.exp2(-x)).astype(BF16)


class _Unit:
    def __init__(self, q_ref, v_ref, z_ref, o_ref, st_ref, rows, h, tri, masks, forward):
        self.q_ref, self.v_ref, self.z_ref, self.o_ref, self.st_ref = q_ref, v_ref, z_ref, o_ref, st_ref
        self.rows, self.h, self.tri, self.masks, self.forward = rows, h, tri, masks, forward
        self.hs = slice(h * HEAD_DIM, (h + 1) * HEAD_DIM)


def _stage_gates(u, lb):
    lb = lb[:, u.hs]
    rest = 1.0 - lb
    z = u.z_ref[0, u.rows, u.hs]
    sig = 1.0 / (1.0 + jnp.exp(-z))
    gate = rest * sig
    g = jnp.log(jnp.maximum(lb + gate, F_FLOOR)) * (1.0 / np.log(2.0))
    u.k = rest - gate
    u.q = u.q_ref[0, u.rows, u.hs] * (HEAD_DIM ** -0.5)
    u.v16 = u.v_ref[0, u.rows, u.hs].astype(BF16)
    u.cs = _dot(u.tri, jnp.concatenate(_bf16_parts(g, CUMSUM_PARTS), axis=-1))


def _stage_levels(u):
    c = u.cs[:, :HEAD_DIM]
    for i in range(1, CUMSUM_PARTS):
        c = c + u.cs[:, i * HEAD_DIM:(i + 1) * HEAD_DIM]
    last = SCAN_CHUNK - 1 if u.forward else 0
    c_last = c[last:last + 1]
    u.q_in = (u.q * jnp.exp2(c)).astype(BF16)
    u.k_st = (u.k * jnp.exp2(c_last - c)).astype(BF16)
    u.e_last = jnp.exp2(c_last)
    u.levels = [_pair_level(u.q, u.k, c, b, u.forward) for b in _scan_levels()]
    u.levels.append(_diag_level(u.q, u.k, c, u.forward))


def _stage_scores(u):
    a = None
    for li, (qd, kd) in enumerate(u.levels):
        a_l = lax.dot_general(qd, kd, (((1,), (1,)), ((), ())), preferred_element_type=F32)
        a = a_l if li == 0 else jnp.where(u.masks[li], a_l, a)
    u.a16 = a.astype(BF16)


def _stage_output(u):
    contract_lanes = (((1,), (1,)), ((), ()))
    contract_rows = (((0,), (0,)), ((), ()))
    st = u.st_ref[u.h]
    o = _dot(u.a16, u.v16)
    o = o + lax.dot_general(u.q_in, st.astype(BF16), contract_lanes, preferred_element_type=F32)
    u.o_ref[0, u.rows, u.hs] = o
    u.st_ref[u.h] = st * u.e_last + lax.dot_general(u.v16, u.k_st, contract_rows,
                                                    preferred_element_type=F32)


def _scan_units(fwd_refs, bwd_refs, tri_f, tri_b):
    masks_f, masks_b = _level_masks(True), _level_masks(False)
    n_chunks = SCAN_BLOCK // SCAN_CHUNK
    units = []
    for ci in range(n_chunks):
        rows_f = slice(ci * SCAN_CHUNK, (ci + 1) * SCAN_CHUNK)
        cb = n_chunks - 1 - ci
        rows_b = slice(cb * SCAN_CHUNK, (cb + 1) * SCAN_CHUNK)
        for h in range(HEADS):
            units.append(_Unit(*fwd_refs, rows_f, h, tri_f, masks_f, True))
        for h in range(HEADS):
            units.append(_Unit(*bwd_refs, rows_b, h, tri_b, masks_b, False))
    return units


def _scan_kernel(lb_ref, mf_ref, mb_ref, qf_ref, vf_ref, zf_ref, qb_ref, vb_ref, zb_ref,
                 of_ref, ob_ref, sf_ref, sb_ref, *, layer):
    @pl.when(pl.program_id(1) == 0)
    def _():
        sf_ref[...] = jnp.zeros_like(sf_ref)
        sb_ref[...] = jnp.zeros_like(sb_ref)

    raw = lb_ref[...]
    ex = jnp.exp(raw - jnp.max(raw, axis=0, keepdims=True))
    soft = ex / jnp.sum(ex, axis=0, keepdims=True)
    lb = jnp.zeros((1, D_MODEL), F32)
    for i in range(1, layer + 1):
        lb = lb + soft[i:i + 1]

    units = _scan_units((qf_ref, vf_ref, zf_ref, of_ref, sf_ref), (qb_ref, vb_ref, zb_ref, ob_ref, sb_ref),
                        mf_ref[...], mb_ref[...])
    n = len(units)
    stages = (functools.partial(_stage_gates, lb=lb), _stage_levels, _stage_scores, _stage_output)
    for i in range(n + SCAN_SKEW[-1]):
        for s, stage in zip(SCAN_SKEW, stages):
            if 0 <= i - s < n:
                stage(units[i - s])


def _scan(proj, hgrn_lb, layer):
    bsz, s, _ = proj.shape
    c = SCAN_BLOCK
    n_steps = s // c
    n_ctx = CTX_LEN // c
    mf = jnp.asarray(_cumsum_matrix(True), BF16)
    mb = jnp.asarray(_cumsum_matrix(False), BF16)

    def fwd_chunk(i):
        return i

    def bwd_chunk(i):
        return jnp.where(i < n_ctx, n_ctx - 1 - i, n_steps - 1 + n_ctx - i)

    def part(chunk_of, col):
        return pl.BlockSpec((1, c, D_MODEL), lambda b, i: (b, chunk_of(i), col))

    out_sds = jax.ShapeDtypeStruct((bsz, s, D_MODEL), F32)
    return pl.pallas_call(
        functools.partial(_scan_kernel, layer=layer),
        grid=(bsz, n_steps),
        in_specs=[
            pl.BlockSpec(hgrn_lb.shape, lambda b, i: (0, 0)),
            pl.BlockSpec(mf.shape, lambda b, i: (0, 0)),
            pl.BlockSpec(mb.shape, lambda b, i: (0, 0)),
            part(fwd_chunk, 0), part(fwd_chunk, 1), part(fwd_chunk, 2),
            part(bwd_chunk, 0), part(bwd_chunk, 1), part(bwd_chunk, 3),
        ],
        out_specs=[
            pl.BlockSpec((1, c, D_MODEL), lambda b, i: (b, fwd_chunk(i), 0)),
            pl.BlockSpec((1, c, D_MODEL), lambda b, i: (b, bwd_chunk(i), 0)),
        ],
        out_shape=[out_sds, out_sds],
        scratch_shapes=[pltpu.VMEM((HEADS, HEAD_DIM, HEAD_DIM), F32),
                        pltpu.VMEM((HEADS, HEAD_DIM, HEAD_DIM), F32)],
        compiler_params=pltpu.CompilerParams(
            dimension_semantics=("arbitrary", "arbitrary"), vmem_limit_bytes=VMEM_LIMIT),
        name="hgrn_scan",
    )(hgrn_lb, mf, mb, proj, proj, proj, proj, proj, proj)


def _readout_kernel(x_ref, of_ref, ob_ref, g_ref, mod_ref, nw_ref, gn_ref, w_ref, o_ref):
    b, i = pl.program_id(0), pl.program_id(1)
    o = of_ref[0] + ob_ref[0]
    g = g_ref[0]
    gate = g * (1.0 / (1.0 + jnp.exp(-g)))
    gn = gn_ref[...]
    heads = []
    for h in range(HEADS):
        hs = slice(h * HEAD_DIM, (h + 1) * HEAD_DIM)
        heads.append(_rms(o[:, hs]) * gn)
    y = jnp.concatenate(heads, axis=-1) * gate
    y = _dot(y.astype(BF16), w_ref[...])
    o_ref[0] = _postnorm_residual(x_ref[0], y, nw_ref[1:2, :], mod_ref, b, i * ROW_TILE, 2)


def _readout(xs, o_f, o_b, proj, mod, nw, gnorm, w):
    bsz, s, d = xs.shape
    tile = pl.BlockSpec((1, ROW_TILE, d), lambda b, i: (b, i, 0))
    return pl.pallas_call(
        _readout_kernel,
        grid=(bsz, s // ROW_TILE),
        in_specs=[
            tile, tile, tile,
            pl.BlockSpec((1, ROW_TILE, d), lambda b, i: (b, i, 4)),
            pl.BlockSpec(mod.shape, lambda b, i: (0, 0)),
            pl.BlockSpec(nw.shape, lambda b, i: (0, 0)),
            pl.BlockSpec(gnorm.shape, lambda b, i: (0, 0)),
            pl.BlockSpec(w.shape, lambda b, i: (0, 0)),
        ],
        out_specs=tile,
        out_shape=jax.ShapeDtypeStruct(xs.shape, F32),
        compiler_params=pltpu.CompilerParams(
            dimension_semantics=("arbitrary", "arbitrary"), vmem_limit_bytes=VMEM_LIMIT),
        name="hgrn_readout",
    )(xs, o_f, o_b, proj, mod, nw, gnorm, w)


def _conv_kernel(x_ref, mod_ref, nw_ref, wb_ref, wc_ref, wx_ref, cw_ref, wo_ref, o_ref):
    b, i = pl.program_id(0), pl.program_id(1)
    row0 = i * ROW_TILE
    x = x_ref[0]
    h = _prenorm(x, nw_ref[0:1, :], mod_ref, b, row0, 0).astype(BF16)
    u = _dot(h, wc_ref[...]) * _dot(h, wx_ref[...])
    pos = row0 + lax.broadcasted_iota(jnp.int32, (ROW_TILE, 1), 0)
    in_ctx = pos < CTX_LEN
    has_prev = (pos != 0) & (in_ctx | (pos % GRID_W != 0))
    has_next = (pos != CTX_LEN - 1) & (in_ctx | (pos % GRID_W != GRID_W - 1))
    u_prev = jnp.where(has_prev, pltpu.roll(u, 1, axis=0), 0.0)
    u_next = jnp.where(has_next, pltpu.roll(u, ROW_TILE - 1, axis=0), 0.0)
    y = u_prev * cw_ref[0:1, :] + u * cw_ref[1:2, :] + u_next * cw_ref[2:3, :]
    y = _dot(h, wb_ref[...]) * y
    y = _dot(y.astype(BF16), wo_ref[...])
    o_ref[0] = _postnorm_residual(x, y, nw_ref[1:2, :], mod_ref, b, row0, 2)


def _conv_mixer(xs, mod, nw, w_in, conv_w, w_out):
    bsz, s, d = xs.shape
    tile = pl.BlockSpec((1, ROW_TILE, d), lambda b, i: (b, i, 0))

    def full(a):
        return pl.BlockSpec(a.shape, lambda b, i: (0, 0))

    return pl.pallas_call(
        _conv_kernel,
        grid=(bsz, s // ROW_TILE),
        in_specs=[
            tile, full(mod), full(nw),
            pl.BlockSpec((d, d), lambda b, i: (0, 0)),
            pl.BlockSpec((d, d), lambda b, i: (0, 1)),
            pl.BlockSpec((d, d), lambda b, i: (0, 2)),
            full(conv_w), full(w_out),
        ],
        out_specs=tile,
        out_shape=jax.ShapeDtypeStruct(xs.shape, F32),
        compiler_params=pltpu.CompilerParams(
            dimension_semantics=("arbitrary", "arbitrary"), vmem_limit_bytes=VMEM_LIMIT),
        name="conv_mixer",
    )(xs, mod, nw, w_in, w_in, w_in, conv_w, w_out)


def _ffn_kernel(x_ref, mod_ref, nw_ref, wg_ref, wu_ref, wo_ref, o_ref, h_ref, acc_ref):
    b, i, j = pl.program_id(0), pl.program_id(1), pl.program_id(2)
    row0 = i * ROW_TILE

    @pl.when(j == 0)
    def _():
        h = _prenorm(x_ref[0], nw_ref[2:3, :], mod_ref, b, row0, 3)
        h_ref[...] = h.astype(BF16)
        acc_ref[...] = jnp.zeros_like(acc_ref)

    h = h_ref[...]
    gate = _dot(h, wg_ref[...])
    up = _dot(h, wu_ref[...])
    act = (gate * (1.0 / (1.0 + jnp.exp(-gate)))) * up
    acc_ref[...] += _dot(act.astype(BF16), wo_ref[...])

    @pl.when(j == pl.num_programs(2) - 1)
    def _():
        o_ref[0] = _postnorm_residual(x_ref[0], acc_ref[...], nw_ref[3:4, :], mod_ref, b, row0, 5)


def _ffn(xs, mod, nw, w_in, w_out):
    bsz, s, d = xs.shape
    hidden = w_out.shape[0]
    tf = hidden // 2
    n_f = hidden // tf
    tile = pl.BlockSpec((1, ROW_TILE, d), lambda b, i, j: (b, i, 0))
    return pl.pallas_call(
        _ffn_kernel,
        grid=(bsz, s // ROW_TILE, n_f),
        in_specs=[
            tile,
            pl.BlockSpec(mod.shape, lambda b, i, j: (0, 0)),
            pl.BlockSpec(nw.shape, lambda b, i, j: (0, 0)),
            pl.BlockSpec((d, tf), lambda b, i, j: (0, j)),
            pl.BlockSpec((d, tf), lambda b, i, j: (0, j + n_f)),
            pl.BlockSpec((tf, d), lambda b, i, j: (j, 0)),
        ],
        out_specs=tile,
        out_shape=jax.ShapeDtypeStruct(xs.shape, F32),
        scratch_shapes=[pltpu.VMEM((ROW_TILE, d), BF16), pltpu.VMEM((ROW_TILE, d), F32)],
        compiler_params=pltpu.CompilerParams(
            dimension_semantics=("arbitrary", "arbitrary", "arbitrary"),
            vmem_limit_bytes=VMEM_LIMIT),
        name="swiglu_ffn",
    )(xs, mod, nw, w_in, w_in, w_out)


def kernel(x, c, ctx, c_ctx, ada_w, ada_b, norm_w, hgrn_w_in, hgrn_w_out, hgrn_gnorm, hgrn_lb,
           conv_w_in, conv_w, conv_w_out, ffn_w_in, ffn_w_out):
    bsz, seq, d = x.shape
    depth = ada_w.shape[0]
    assert d == D_MODEL and ctx.shape[1] == CTX_LEN and bsz < MOD_ROWS
    assert (seq + CTX_LEN) % ROW_TILE == 0 and ROW_TILE % CTX_LEN == 0 and seq % GRID_W == 0

    c_rows = jnp.zeros((MOD_ROWS, d), F32).at[:bsz].set(c).at[MOD_ROWS - 1].set(c_ctx)
    mod_all = _ada_table(c_rows, ada_w, ada_b)

    xs = jnp.concatenate([ctx, x], axis=1)
    for l in range(depth):
        j = l // 2
        mod, nw = mod_all[l], norm_w[l]
        if l % 2 == 0:
            proj = _in_proj(xs, mod, nw, hgrn_w_in[j].astype(BF16))
            o_f, o_b = _scan(proj, hgrn_lb, j)
            gnorm = hgrn_gnorm[j].reshape(1, HEAD_DIM)
            xs = _readout(xs, o_f, o_b, proj, mod, nw, gnorm, hgrn_w_out[j].astype(BF16))
        else:
            xs = _conv_mixer(xs, mod, nw, conv_w_in[j].astype(BF16), conv_w[j],
                             conv_w_out[j].astype(BF16))
        xs = _ffn(xs, mod, nw, ffn_w_in[l].astype(BF16), ffn_w_out[l].astype(BF16))
    return xs[:, CTX_LEN:, :]
```

```python
import functools

import numpy as np
import jax
import jax.numpy as jnp
from jax import lax
from jax.experimental import pallas as pl
from jax.experimental.pallas import tpu as pltpu

D_MODEL = 1024
CTX_LEN = 256
GRID_W = 64
HEADS = 8
HEAD_DIM = D_MODEL // HEADS
CONV_WIDTH = 3
N_MOD = 6
EPS = 1e-6
F_FLOOR = 1e-6

SCAN_BLOCK = 256
SCAN_CHUNK = 128
SCAN_BASE = 8
SCAN_SKEW = (0, 1, 2, 3)
CUMSUM_PARTS = 2
ROW_TILE = 768
SUB_ROWS = CTX_LEN
FFN_SPLIT = 2
MOD_ROWS = 8
VMEM_LIMIT = 56 * 1024 * 1024

F32 = jnp.float32
BF16 = jnp.bfloat16


def _bf16_parts(x, n):
    parts = []
    r = x
    for i in range(n):
        p = r.astype(BF16)
        parts.append(p)
        if i + 1 < n:
            r = r - p.astype(F32)
    return parts


def _dot(a, b):
    return jnp.dot(a, b, preferred_element_type=F32)


def _rms(x):
    return x * lax.rsqrt(jnp.mean(x * x, axis=-1, keepdims=True) + EPS)


def _silu(x):
    return x * (1.0 / (1.0 + jnp.exp(-x)))


def _emit_pipeline(n_items, stages):
    depth = max(off for off, _ in stages)
    for t in range(n_items + depth):
        for off, fn in stages:
            if 0 <= t - off < n_items:
                fn(t - off)


def _sub_blocks(tile_idx, n_tiles):
    n = ROW_TILE // SUB_ROWS
    blocks = []
    for r in range(n):
        rows = slice(r * SUB_ROWS, (r + 1) * SUB_ROWS)
        blocks.append((rows, (tile_idx == n_tiles - 1) if r == n - 1 else False))
    return blocks


def _mod_vec(mod_ref, batch, is_ctx, idx):
    row = batch if is_ctx is False else jnp.where(is_ctx, MOD_ROWS - 1, batch)
    return mod_ref[pl.ds(row, 1), idx * D_MODEL:(idx + 1) * D_MODEL]


def _prenorm(x, nw, mod_ref, batch, is_ctx, shift_idx):
    shift = _mod_vec(mod_ref, batch, is_ctx, shift_idx)
    scale = _mod_vec(mod_ref, batch, is_ctx, shift_idx + 1)
    return _rms(x) * (nw * (1.0 + scale)) + shift


def _postnorm_residual(x, y, nw, mod_ref, batch, is_ctx, gate_idx):
    gate = _mod_vec(mod_ref, batch, is_ctx, gate_idx)
    return x + _rms(y) * (gate * nw)


def _params(n_axes):
    return pltpu.CompilerParams(dimension_semantics=("arbitrary",) * n_axes,
                                vmem_limit_bytes=VMEM_LIMIT)


def _resident(a):
    zeros = (0,) * a.ndim
    return pl.BlockSpec(a.shape, lambda *_: zeros, pipeline_mode=pl.Buffered(1))


def _ada_kernel(c_ref, w_ref, b_ref, o_ref):
    a_hi, a_lo = _bf16_parts(_silu(c_ref[...]), 2)
    w_hi, w_lo = _bf16_parts(w_ref[0], 2)
    o_ref[0] = _dot(a_hi, w_hi) + _dot(a_lo, w_hi) + _dot(a_hi, w_lo) + b_ref[0]


def _ada_table(c_rows, ada_w, ada_b):
    depth, d, n = ada_w.shape
    tn = 1536
    return pl.pallas_call(
        _ada_kernel,
        grid=(depth, n // tn),
        in_specs=[
            pl.BlockSpec((MOD_ROWS, d), lambda l, j: (0, 0)),
            pl.BlockSpec((1, d, tn), lambda l, j: (l, 0, j)),
            pl.BlockSpec((1, 1, tn), lambda l, j: (l, 0, j)),
        ],
        out_specs=pl.BlockSpec((1, MOD_ROWS, tn), lambda l, j: (l, 0, j)),
        out_shape=jax.ShapeDtypeStruct((depth, MOD_ROWS, n), F32),
        compiler_params=_params(2),
        name="ada_table",
    )(c_rows, ada_w, ada_b.reshape(depth, 1, n))


def _in_proj_kernel(*refs, joins, n_tiles):
    if joins:
        x_ref, ctx_ref, mod_ref, nw_ref, w_ref, o_ref, xs_ref, h_ref = refs
    else:
        x_ref, mod_ref, nw_ref, w_ref, o_ref, h_ref = refs
    b, i, j = pl.program_id(0), pl.program_id(1), pl.program_id(2)

    @pl.when(j == 0)
    def _():
        def take(src, rows, is_ctx):
            x = src if rows is None else src[0, rows]
            h_ref[out_rows] = _prenorm(x, nw_ref[0:1, :], mod_ref, b, is_ctx, 0).astype(BF16)
            if joins:
                xs_ref[0, out_rows] = x

        for out_rows, is_ctx in _sub_blocks(i, n_tiles):
            if joins and is_ctx is not False:
                pl.when(is_ctx)(functools.partial(take, ctx_ref[0], None, True))
                pl.when(jnp.logical_not(is_ctx))(functools.partial(take, x_ref, out_rows, False))
            else:
                take(x_ref, out_rows, is_ctx)

    o_ref[0] = _dot(h_ref[...], w_ref[...])


def _in_proj(x, ctx, mod, nw, w):
    bsz, _, d = x.shape
    joins = ctx is not None
    s = x.shape[1] + (CTX_LEN if joins else 0)
    n_tiles = s // ROW_TILE
    n = w.shape[1]
    tn = 1024
    tile = pl.BlockSpec((1, ROW_TILE, d), lambda b, i, j: (b, i, 0))

    def full(a):
        return pl.BlockSpec(a.shape, lambda b, i, j: (0, 0))

    in_specs = [tile] + ([pl.BlockSpec((1, CTX_LEN, d), lambda b, i, j: (b, 0, 0))] if joins else [])
    in_specs += [full(mod), full(nw), pl.BlockSpec((d, tn), lambda b, i, j: (0, j))]
    proj_spec = pl.BlockSpec((1, ROW_TILE, tn), lambda b, i, j: (b, i, j))
    proj_sds = jax.ShapeDtypeStruct((bsz, s, n), F32)
    out = pl.pallas_call(
        functools.partial(_in_proj_kernel, joins=joins, n_tiles=n_tiles),
        grid=(bsz, n_tiles, n // tn),
        in_specs=in_specs,
        out_specs=[proj_spec, tile] if joins else proj_spec,
        out_shape=[proj_sds, jax.ShapeDtypeStruct((bsz, s, d), F32)] if joins else proj_sds,
        scratch_shapes=[pltpu.VMEM((ROW_TILE, d), BF16)],
        compiler_params=_params(3),
        name="hgrn_in_proj",
    )(*([x, ctx] if joins else [x]), mod, nw, w)
    return out if joins else (out, x)


def _scan_levels():
    levels = []
    b = SCAN_CHUNK // 2
    while b >= SCAN_BASE:
        levels.append(b)
        b //= 2
    return levels


def _cumsum_matrix(forward):
    t = np.arange(SCAN_CHUNK)
    tri = t[None, :] <= t[:, None] if forward else t[None, :] >= t[:, None]
    return tri.astype(np.float32)


def _level_masks(forward):
    c = SCAN_CHUNK
    t = lax.broadcasted_iota(jnp.int32, (c, c), 0)
    s = lax.broadcasted_iota(jnp.int32, (c, c), 1)
    if not forward:
        t, s = (c - 1) - t, (c - 1) - s
    masks = []
    for b in _scan_levels():
        same = (t // (2 * b)) == (s // (2 * b))
        masks.append(same & (t % (2 * b) >= b) & (s % (2 * b) < b))
    same = (t // SCAN_BASE) == (s // SCAN_BASE)
    masks.append(same & (s <= t))
    return masks


def _pair_level(q, k, c, half, forward):
    packed_rows = 16
    qs, ks = [], []
    for start in range(0, SCAN_CHUNK, 2 * half):
        lo, hi = slice(start, start + half), slice(start + half, start + 2 * half)
        if forward:
            ref, q_rows, k_rows = c[start + half - 1:start + half], hi, lo
        else:
            ref, q_rows, k_rows = c[start + half:start + half + 1], lo, hi
        eq = jnp.exp2(c[q_rows] - ref)
        ek = jnp.exp2(ref - c[k_rows])
        if half % packed_rows == 0:
            zero = jnp.zeros((half, c.shape[-1]), BF16)
            qe, ke = (q[q_rows] * eq).astype(BF16), (k[k_rows] * ek).astype(BF16)
            qs += [zero, qe] if forward else [qe, zero]
            ks += [ke, zero] if forward else [zero, ke]
        else:
            e = jnp.concatenate([ek, eq] if forward else [eq, ek], axis=0)
            rows = slice(start, start + 2 * half)
            qs.append((q[rows] * e).astype(BF16))
            ks.append((k[rows] * e).astype(BF16))
    return jnp.concatenate(qs, axis=0), jnp.concatenate(ks, axis=0)


def _diag_level(q, k, c, forward):
    c3 = c.reshape(SCAN_CHUNK // SCAN_BASE, SCAN_BASE, c.shape[-1])
    r = SCAN_BASE // 2 - 1 if forward else SCAN_BASE // 2
    x = (c3 - c3[:, r:r + 1, :]).reshape(c.shape)
    return (q * jnp.exp2(x)).astype(BF16), (k * jnp.exp2(-x)).astype(BF16)


class _Unit:
    def __init__(self, q_ref, v_ref, z_ref, o_ref, st_ref, rows, h, tri, masks, forward):
        self.q_ref, self.v_ref, self.z_ref, self.o_ref, self.st_ref = q_ref, v_ref, z_ref, o_ref, st_ref
        self.rows, self.h, self.tri, self.masks, self.forward = rows, h, tri, masks, forward
        self.hs = slice(h * HEAD_DIM, (h + 1) * HEAD_DIM)


def _stage_gates(u, lb):
    lb = lb[:, u.hs]
    rest = 1.0 - lb
    z = u.z_ref[0, u.rows, u.hs]
    sig = 1.0 / (1.0 + jnp.exp(-z))
    gate = rest * sig
    g = jnp.log(jnp.maximum(lb + gate, F_FLOOR)) * (1.0 / np.log(2.0))
    u.k = rest - gate
    u.q = u.q_ref[0, u.rows, u.hs] * (HEAD_DIM ** -0.5)
    u.v16 = u.v_ref[0, u.rows, u.hs].astype(BF16)
    u.cs = _dot(u.tri, jnp.concatenate(_bf16_parts(g, CUMSUM_PARTS), axis=-1))


def _stage_levels(u):
    c = u.cs[:, :HEAD_DIM]
    for i in range(1, CUMSUM_PARTS):
        c = c + u.cs[:, i * HEAD_DIM:(i + 1) * HEAD_DIM]
    last = SCAN_CHUNK - 1 if u.forward else 0
    c_last = c[last:last + 1]
    u.q_in = (u.q * jnp.exp2(c)).astype(BF16)
    u.k_st = (u.k * jnp.exp2(c_last - c)).astype(BF16)
    u.e_last = jnp.exp2(c_last)
    u.levels = [_pair_level(u.q, u.k, c, b, u.forward) for b in _scan_levels()]
    u.levels.append(_diag_level(u.q, u.k, c, u.forward))


def _stage_scores(u):
    a = None
    for li, (qd, kd) in enumerate(u.levels):
        a_l = lax.dot_general(qd, kd, (((1,), (1,)), ((), ())), preferred_element_type=F32)
        a = a_l if li == 0 else jnp.where(u.masks[li], a_l, a)
    u.a16 = a.astype(BF16)


def _stage_output(u):
    contract_lanes = (((1,), (1,)), ((), ()))
    contract_rows = (((0,), (0,)), ((), ()))
    st = u.st_ref[u.h]
    o = _dot(u.a16, u.v16)
    o = o + lax.dot_general(u.q_in, st.astype(BF16), contract_lanes, preferred_element_type=F32)
    u.o_ref[0, u.rows, u.hs] = o
    u.st_ref[u.h] = st * u.e_last + lax.dot_general(u.v16, u.k_st, contract_rows,
                                                    preferred_element_type=F32)


def _scan_units(fwd_refs, bwd_refs, tri_f, tri_b):
    masks_f, masks_b = _level_masks(True), _level_masks(False)
    n_chunks = SCAN_BLOCK // SCAN_CHUNK
    units = []
    for ci in range(n_chunks):
        rows_f = slice(ci * SCAN_CHUNK, (ci + 1) * SCAN_CHUNK)
        cb = n_chunks - 1 - ci
        rows_b = slice(cb * SCAN_CHUNK, (cb + 1) * SCAN_CHUNK)
        for h in range(HEADS):
            units.append(_Unit(*fwd_refs, rows_f, h, tri_f, masks_f, True))
        for h in range(HEADS):
            units.append(_Unit(*bwd_refs, rows_b, h, tri_b, masks_b, False))
    return units


def _scan_kernel(lb_ref, mf_ref, mb_ref, qf_ref, vf_ref, zf_ref, qb_ref, vb_ref, zb_ref,
                 of_ref, ob_ref, sf_ref, sb_ref, *, layer):
    @pl.when(pl.program_id(1) == 0)
    def _():
        sf_ref[...] = jnp.zeros_like(sf_ref)
        sb_ref[...] = jnp.zeros_like(sb_ref)

    raw = lb_ref[...]
    ex = jnp.exp(raw - jnp.max(raw, axis=0, keepdims=True))
    soft = ex / jnp.sum(ex, axis=0, keepdims=True)
    lb = jnp.zeros((1, D_MODEL), F32)
    for i in range(1, layer + 1):
        lb = lb + soft[i:i + 1]

    units = _scan_units((qf_ref, vf_ref, zf_ref, of_ref, sf_ref), (qb_ref, vb_ref, zb_ref, ob_ref, sb_ref),
                        mf_ref[...], mb_ref[...])
    stages = (functools.partial(_stage_gates, lb=lb), _stage_levels, _stage_scores, _stage_output)
    _emit_pipeline(len(units), [(off, lambda n, fn=fn: fn(units[n])) for off, fn in zip(SCAN_SKEW, stages)])


def _scan(proj, hgrn_lb, layer):
    bsz, s, _ = proj.shape
    c = SCAN_BLOCK
    n_steps = s // c
    n_ctx = CTX_LEN // c
    mf = jnp.asarray(_cumsum_matrix(True), BF16)
    mb = jnp.asarray(_cumsum_matrix(False), BF16)

    def fwd_block(i):
        return jnp.where(i < n_ctx, n_steps - n_ctx + i, i - n_ctx)

    def bwd_block(i):
        return n_steps - 1 - i

    def part(block_of, col):
        return pl.BlockSpec((1, c, D_MODEL), lambda b, i: (b, block_of(i), col))

    out_sds = jax.ShapeDtypeStruct((bsz, s, D_MODEL), F32)
    return pl.pallas_call(
        functools.partial(_scan_kernel, layer=layer),
        grid=(bsz, n_steps),
        in_specs=[
            pl.BlockSpec(hgrn_lb.shape, lambda b, i: (0, 0)),
            pl.BlockSpec(mf.shape, lambda b, i: (0, 0)),
            pl.BlockSpec(mb.shape, lambda b, i: (0, 0)),
            part(fwd_block, 0), part(fwd_block, 1), part(fwd_block, 2),
            part(bwd_block, 0), part(bwd_block, 1), part(bwd_block, 3),
        ],
        out_specs=[part(fwd_block, 0), part(bwd_block, 0)],
        out_shape=[out_sds, out_sds],
        scratch_shapes=[pltpu.VMEM((HEADS, HEAD_DIM, HEAD_DIM), F32),
                        pltpu.VMEM((HEADS, HEAD_DIM, HEAD_DIM), F32)],
        compiler_params=_params(2),
        name="hgrn_scan",
    )(hgrn_lb, mf, mb, proj, proj, proj, proj, proj, proj)


def _readout_kernel(x_ref, of_ref, ob_ref, g_ref, mod_ref, nw_ref, gn_ref, w_ref, o_ref, *, n_tiles):
    b, i = pl.program_id(0), pl.program_id(1)
    blocks = _sub_blocks(i, n_tiles)
    ys = [None] * len(blocks)

    def mix(n):
        rows, _ = blocks[n]
        o = of_ref[0, rows] + ob_ref[0, rows]
        gn = gn_ref[...]
        heads = [_rms(o[:, h * HEAD_DIM:(h + 1) * HEAD_DIM]) * gn for h in range(HEADS)]
        y = jnp.concatenate(heads, axis=-1) * _silu(g_ref[0, rows])
        ys[n] = _dot(y.astype(BF16), w_ref[...])

    def finish(n):
        rows, is_ctx = blocks[n]
        o_ref[0, rows] = _postnorm_residual(x_ref[0, rows], ys[n], nw_ref[1:2, :], mod_ref, b, is_ctx, 2)

    _emit_pipeline(len(blocks), [(0, mix), (1, finish)])


def _readout(xs, o_f, o_b, proj, mod, nw, gnorm, w):
    bsz, s, d = xs.shape
    n_tiles = s // ROW_TILE
    tile = pl.BlockSpec((1, ROW_TILE, d), lambda b, i: (b, i, 0))

    def full(a):
        return pl.BlockSpec(a.shape, lambda b, i: (0, 0))

    return pl.pallas_call(
        functools.partial(_readout_kernel, n_tiles=n_tiles),
        grid=(bsz, n_tiles),
        in_specs=[tile, tile, tile, pl.BlockSpec((1, ROW_TILE, d), lambda b, i: (b, i, 4)),
                  full(mod), full(nw), full(gnorm), full(w)],
        out_specs=tile,
        out_shape=jax.ShapeDtypeStruct(xs.shape, F32),
        compiler_params=_params(2),
        name="hgrn_readout",
    )(xs, o_f, o_b, proj, mod, nw, gnorm, w)


def _conv_kernel(x_ref, mod_ref, nw_ref, wi_ref, cw_ref, wo_ref, o_ref, *, n_tiles, seq):
    b, i = pl.program_id(0), pl.program_id(1)
    d = D_MODEL
    blocks = _sub_blocks(i, n_tiles)
    hs, us, gs, ys = ([None] * len(blocks) for _ in range(4))

    def pre(n):
        rows, is_ctx = blocks[n]
        hs[n] = _prenorm(x_ref[0, rows], nw_ref[0:1, :], mod_ref, b, is_ctx, 0).astype(BF16)

    def project(n):
        gs[n] = _dot(hs[n], wi_ref[:, 0:d])
        us[n] = _dot(hs[n], wi_ref[:, d:2 * d]) * _dot(hs[n], wi_ref[:, 2 * d:3 * d])

    def convolve(n):
        rows, _ = blocks[n]
        u = us[n]
        pos = i * ROW_TILE + rows.start + lax.broadcasted_iota(jnp.int32, (SUB_ROWS, 1), 0)
        in_ctx = pos >= seq
        has_prev = (pos != seq) & (in_ctx | (pos % GRID_W != 0))
        has_next = (pos != seq + CTX_LEN - 1) & (in_ctx | (pos % GRID_W != GRID_W - 1))
        u_prev = jnp.where(has_prev, pltpu.roll(u, 1, axis=0), 0.0)
        u_next = jnp.where(has_next, pltpu.roll(u, SUB_ROWS - 1, axis=0), 0.0)
        y = u_prev * cw_ref[0:1, :] + u * cw_ref[1:2, :] + u_next * cw_ref[2:3, :]
        ys[n] = _dot((gs[n] * y).astype(BF16), wo_ref[...])

    def finish(n):
        rows, is_ctx = blocks[n]
        o_ref[0, rows] = _postnorm_residual(x_ref[0, rows], ys[n], nw_ref[1:2, :], mod_ref, b, is_ctx, 2)

    _emit_pipeline(len(blocks), [(0, pre), (1, project), (2, convolve), (3, finish)])


def _conv_mixer(xs, mod, nw, w_in, conv_w, w_out):
    bsz, s, d = xs.shape
    n_tiles = s // ROW_TILE
    tile = pl.BlockSpec((1, ROW_TILE, d), lambda b, i: (b, i, 0))

    def full(a):
        return pl.BlockSpec(a.shape, lambda b, i: (0, 0))

    return pl.pallas_call(
        functools.partial(_conv_kernel, n_tiles=n_tiles, seq=s - CTX_LEN),
        grid=(bsz, n_tiles),
        in_specs=[tile, full(mod), full(nw), _resident(w_in), full(conv_w), _resident(w_out)],
        out_specs=tile,
        out_shape=jax.ShapeDtypeStruct(xs.shape, F32),
        compiler_params=_params(2),
        name="conv_mixer",
    )(xs, mod, nw, w_in, conv_w, w_out)


def _ffn_kernel(x_ref, mod_ref, nw_ref, wi_ref, wo_ref, o_ref, *, n_tiles):
    b, i = pl.program_id(0), pl.program_id(1)
    hidden = wo_ref.shape[0]
    tf = hidden // FFN_SPLIT
    blocks = _sub_blocks(i, n_tiles)
    n_items = len(blocks) * FFN_SPLIT
    hs, accs = [None] * len(blocks), [None] * len(blocks)
    gates, ups, acts = ([None] * n_items for _ in range(3))

    def pre(n):
        r, f = divmod(n, FFN_SPLIT)
        if f == 0:
            rows, is_ctx = blocks[r]
            hs[r] = _prenorm(x_ref[0, rows], nw_ref[2:3, :], mod_ref, b, is_ctx, 3).astype(BF16)

    def project(n):
        r, f = divmod(n, FFN_SPLIT)
        gates[n] = _dot(hs[r], wi_ref[:, f * tf:(f + 1) * tf])
        ups[n] = _dot(hs[r], wi_ref[:, hidden + f * tf:hidden + (f + 1) * tf])

    def activate(n):
        acts[n] = (_silu(gates[n]) * ups[n]).astype(BF16)
        gates[n] = ups[n] = None

    def down(n):
        r, f = divmod(n, FFN_SPLIT)
        y = _dot(acts[n], wo_ref[f * tf:(f + 1) * tf, :])
        accs[r] = y if f == 0 else accs[r] + y
        acts[n] = None

    def finish(n):
        r, f = divmod(n, FFN_SPLIT)
        if f == FFN_SPLIT - 1:
            rows, is_ctx = blocks[r]
            o_ref[0, rows] = _postnorm_residual(x_ref[0, rows], accs[r], nw_ref[3:4, :],
                                                mod_ref, b, is_ctx, 5)

    _emit_pipeline(n_items, [(0, pre), (1, project), (2, activate), (3, down), (4, finish)])


def _ffn(xs, mod, nw, w_in, w_out, out_rows):
    bsz, s, d = xs.shape
    n_tiles = s // ROW_TILE
    tile = pl.BlockSpec((1, ROW_TILE, d), lambda b, i: (b, i, 0))

    def full(a):
        return pl.BlockSpec(a.shape, lambda b, i: (0, 0))

    return pl.pallas_call(
        functools.partial(_ffn_kernel, n_tiles=n_tiles),
        grid=(bsz, n_tiles),
        in_specs=[tile, full(mod), full(nw), _resident(w_in), _resident(w_out)],
        out_specs=tile,
        out_shape=jax.ShapeDtypeStruct((bsz, out_rows, d), F32),
        compiler_params=_params(2),
        name="swiglu_ffn",
    )(xs, mod, nw, w_in, w_out)


def kernel(x, c, ctx, c_ctx, ada_w, ada_b, norm_w, hgrn_w_in, hgrn_w_out, hgrn_gnorm, hgrn_lb,
           conv_w_in, conv_w, conv_w_out, ffn_w_in, ffn_w_out):
    bsz, seq, d = x.shape
    depth = ada_w.shape[0]
    s_all = seq + CTX_LEN
    assert d == D_MODEL and ctx.shape[1] == CTX_LEN and bsz < MOD_ROWS and depth % 2 == 0
    assert s_all % ROW_TILE == 0 and ROW_TILE % SUB_ROWS == 0 and SUB_ROWS % GRID_W == 0
    assert seq % SCAN_BLOCK == 0 and CTX_LEN % SCAN_BLOCK == 0 and seq % GRID_W == 0

    c_rows = jnp.zeros((MOD_ROWS, d), F32).at[:bsz].set(c).at[MOD_ROWS - 1].set(c_ctx)
    mod_all = _ada_table(c_rows, ada_w, ada_b)

    xs = None
    for l in range(depth):
        j = l // 2
        mod, nw = mod_all[l], norm_w[l]
        if l % 2 == 0:
            w_in = hgrn_w_in[j].astype(BF16)
            proj, xs = _in_proj(x, ctx, mod, nw, w_in) if xs is None else _in_proj(xs, None, mod, nw, w_in)
            o_f, o_b = _scan(proj, hgrn_lb, j)
            gnorm = hgrn_gnorm[j].reshape(1, HEAD_DIM)
            xs = _readout(xs, o_f, o_b, proj, mod, nw, gnorm, hgrn_w_out[j].astype(BF16))
        else:
            xs = _conv_mixer(xs, mod, nw, conv_w_in[j].astype(BF16), conv_w[j],
                             conv_w_out[j].astype(BF16))
        out_rows = seq if l == depth - 1 else s_all
        xs = _ffn(xs, mod, nw, ffn_w_in[l].astype(BF16), ffn_w_out[l].astype(BF16), out_rows)
    return xs
```

```python
import functools

import numpy as np
import jax
import jax.numpy as jnp
from jax import lax
from jax.experimental import pallas as pl
from jax.experimental.pallas import tpu as pltpu

D_MODEL = 1024
CTX_LEN = 256
GRID_W = 64
HEADS = 8
HEAD_DIM = D_MODEL // HEADS
CONV_WIDTH = 3
N_MOD = 6
EPS = 1e-6
F_FLOOR = 1e-6

SCAN_BLOCK = 256
SCAN_CHUNK = 128
SCAN_BASE = 8
SCAN_SKEW = (0, 1, 2, 3)
CUMSUM_PARTS = 2
ROW_TILE = 768
SUB_ROWS = CTX_LEN
FFN_SPLIT = 2
MOD_ROWS = 8
VMEM_LIMIT = 56 * 1024 * 1024

F32 = jnp.float32
BF16 = jnp.bfloat16


def _bf16_parts(x, n):
    parts = []
    r = x
    for i in range(n):
        p = r.astype(BF16)
        parts.append(p)
        if i + 1 < n:
            r = r - p.astype(F32)
    return parts


def _dot(a, b):
    return jnp.dot(a, b, preferred_element_type=F32)


def _rms(x):
    return x * lax.rsqrt(jnp.mean(x * x, axis=-1, keepdims=True) + EPS)


def _silu(x):
    return x * (1.0 / (1.0 + jnp.exp(-x)))


def _emit_pipeline(n_items, stages):
    depth = max(off for off, _ in stages)
    for t in range(n_items + depth):
        for off, fn in stages:
            if 0 <= t - off < n_items:
                fn(t - off)


def _sub_blocks(tile_idx, n_tiles):
    n = ROW_TILE // SUB_ROWS
    blocks = []
    for r in range(n):
        rows = slice(r * SUB_ROWS, (r + 1) * SUB_ROWS)
        blocks.append((rows, (tile_idx == n_tiles - 1) if r == n - 1 else False))
    return blocks


def _mod_vec(mod_ref, batch, is_ctx, idx):
    row = batch if is_ctx is False else jnp.where(is_ctx, MOD_ROWS - 1, batch)
    return mod_ref[pl.ds(row, 1), idx * D_MODEL:(idx + 1) * D_MODEL]


def _prenorm(x, nw, mod_ref, batch, is_ctx, shift_idx):
    shift = _mod_vec(mod_ref, batch, is_ctx, shift_idx)
    scale = _mod_vec(mod_ref, batch, is_ctx, shift_idx + 1)
    return _rms(x) * (nw * (1.0 + scale)) + shift


def _postnorm_residual(x, y, nw, mod_ref, batch, is_ctx, gate_idx):
    gate = _mod_vec(mod_ref, batch, is_ctx, gate_idx)
    return x + _rms(y) * (gate * nw)


def _params(n_axes):
    return pltpu.CompilerParams(dimension_semantics=("arbitrary",) * n_axes,
                                vmem_limit_bytes=VMEM_LIMIT)


def _resident(a, layer=None):
    if layer is None:
        shape, index = a.shape, (0,) * a.ndim
    else:
        shape, index = (None,) + a.shape[1:], (layer,) + (0,) * (a.ndim - 1)
    return pl.BlockSpec(shape, lambda *_: index, pipeline_mode=pl.Buffered(1))


def _ada_kernel(c_ref, w_ref, b_ref, o_ref):
    a_hi, a_lo = _bf16_parts(_silu(c_ref[...]), 2)
    w_hi, w_lo = _bf16_parts(w_ref[0], 2)
    o_ref[0] = _dot(a_hi, w_hi) + _dot(a_lo, w_hi) + _dot(a_hi, w_lo) + b_ref[0]


def _ada_table(c_rows, ada_w, ada_b):
    depth, d, n = ada_w.shape
    tn = 1536
    return pl.pallas_call(
        _ada_kernel,
        grid=(depth, n // tn),
        in_specs=[
            pl.BlockSpec((MOD_ROWS, d), lambda l, j: (0, 0)),
            pl.BlockSpec((1, d, tn), lambda l, j: (l, 0, j)),
            pl.BlockSpec((1, 1, tn), lambda l, j: (l, 0, j)),
        ],
        out_specs=pl.BlockSpec((1, MOD_ROWS, tn), lambda l, j: (l, 0, j)),
        out_shape=jax.ShapeDtypeStruct((depth, MOD_ROWS, n), F32),
        compiler_params=_params(2),
        name="ada_table",
    )(c_rows, ada_w, ada_b.reshape(depth, 1, n))


def _in_proj_kernel(*refs, joins, n_tiles):
    n_blk = ROW_TILE // SUB_ROWS
    x_refs, refs = refs[:n_blk], refs[n_blk:]
    ctx_ref = refs[0] if joins else None
    mod_ref, nw_ref, w_ref, q_ref, v_ref, zf_ref, zb_ref, g_ref = refs[joins:joins + 8]
    xs_ref = refs[-1] if joins else None
    b, i = pl.program_id(0), pl.program_id(1)
    d = D_MODEL
    blocks = _sub_blocks(i, n_tiles)
    hs = [None] * n_blk

    def pre(n):
        rows, is_ctx = blocks[n]
        x = x_refs[n][0]
        if joins:
            if is_ctx is not False:
                x = jnp.where(is_ctx, ctx_ref[0], x)
            xs_ref[0, rows] = x
        hs[n] = _prenorm(x, nw_ref[0:1, :], mod_ref, b, is_ctx, 0).astype(BF16)

    def project(n):
        rows, _ = blocks[n]
        h = hs[n]
        q_ref[0, rows] = (_dot(h, w_ref[:, 0:d]) * (HEAD_DIM ** -0.5)).astype(BF16)
        v_ref[0, rows] = _dot(h, w_ref[:, d:2 * d]).astype(BF16)
        zf_ref[0, rows] = _dot(h, w_ref[:, 2 * d:3 * d])
        zb_ref[0, rows] = _dot(h, w_ref[:, 3 * d:4 * d])
        g_ref[0, rows] = _dot(h, w_ref[:, 4 * d:5 * d]).astype(BF16)

    _emit_pipeline(n_blk, [(0, pre), (1, project)])


def _in_proj(x, ctx, mod_all, norm_w, w_all, layer, j):
    bsz, _, d = x.shape
    joins = ctx is not None
    s = x.shape[1] + (CTX_LEN if joins else 0)
    n_tiles = s // ROW_TILE
    n_blk = ROW_TILE // SUB_ROWS
    last_block = x.shape[1] // SUB_ROWS - 1
    tile = pl.BlockSpec((1, ROW_TILE, d), lambda b, i: (b, i, 0))

    def x_block(n):
        return pl.BlockSpec((1, SUB_ROWS, d), lambda b, i: (b, jnp.minimum(i * n_blk + n, last_block), 0))

    in_specs = [x_block(n) for n in range(n_blk)]
    in_specs += [pl.BlockSpec((1, CTX_LEN, d), lambda b, i: (b, 0, 0))] if joins else []
    in_specs += [_resident(mod_all, layer), _resident(norm_w, layer), _resident(w_all, j)]
    sds16 = jax.ShapeDtypeStruct((bsz, s, d), BF16)
    sds32 = jax.ShapeDtypeStruct((bsz, s, d), F32)
    out = pl.pallas_call(
        functools.partial(_in_proj_kernel, joins=joins, n_tiles=n_tiles),
        grid=(bsz, n_tiles),
        in_specs=in_specs,
        out_specs=[tile] * (6 if joins else 5),
        out_shape=[sds16, sds16, sds32, sds32, sds16] + ([sds32] if joins else []),
        compiler_params=_params(2),
        name="hgrn_in_proj",
    )(*([x] * n_blk), *([ctx] if joins else []), mod_all, norm_w, w_all)
    return (out[:5], out[5]) if joins else (out, x)


def _scan_levels():
    levels = []
    b = SCAN_CHUNK // 2
    while b >= SCAN_BASE:
        levels.append(b)
        b //= 2
    return levels


def _cumsum_matrix(forward):
    t = np.arange(SCAN_CHUNK)
    tri = t[None, :] <= t[:, None] if forward else t[None, :] >= t[:, None]
    return tri.astype(np.float32)


def _level_masks(forward):
    c = SCAN_CHUNK
    t = lax.broadcasted_iota(jnp.int32, (c, c), 0)
    s = lax.broadcasted_iota(jnp.int32, (c, c), 1)
    if not forward:
        t, s = (c - 1) - t, (c - 1) - s
    masks = []
    for b in _scan_levels():
        same = (t // (2 * b)) == (s // (2 * b))
        masks.append(same & (t % (2 * b) >= b) & (s % (2 * b) < b))
    same = (t // SCAN_BASE) == (s // SCAN_BASE)
    masks.append(same & (s <= t))
    return masks


def _pair_level(q, k, c, half, forward):
    packed_rows = 16
    qs, ks = [], []
    for start in range(0, SCAN_CHUNK, 2 * half):
        lo, hi = slice(start, start + half), slice(start + half, start + 2 * half)
        if forward:
            ref, q_rows, k_rows = c[start + half - 1:start + half], hi, lo
        else:
            ref, q_rows, k_rows = c[start + half:start + half + 1], lo, hi
        eq = jnp.exp2(c[q_rows] - ref)
        ek = jnp.exp2(ref - c[k_rows])
        if half % packed_rows == 0:
            zero = jnp.zeros((half, c.shape[-1]), BF16)
            qe, ke = (q[q_rows] * eq).astype(BF16), (k[k_rows] * ek).astype(BF16)
            qs += [zero, qe] if forward else [qe, zero]
            ks += [ke, zero] if forward else [zero, ke]
        else:
            e = jnp.concatenate([ek, eq] if forward else [eq, ek], axis=0)
            rows = slice(start, start + 2 * half)
            qs.append((q[rows] * e).astype(BF16))
            ks.append((k[rows] * e).astype(BF16))
    return jnp.concatenate(qs, axis=0), jnp.concatenate(ks, axis=0)


def _diag_level(q, k, c, forward):
    c3 = c.reshape(SCAN_CHUNK // SCAN_BASE, SCAN_BASE, c.shape[-1])
    r = SCAN_BASE // 2 - 1 if forward else SCAN_BASE // 2
    x = (c3 - c3[:, r:r + 1, :]).reshape(c.shape)
    return (q * jnp.exp2(x)).astype(BF16), (k * jnp.exp2(-x)).astype(BF16)


class _Unit:
    def __init__(self, q_ref, v_ref, z_ref, o_ref, st_ref, rows, h, tri, masks, forward):
        self.q_ref, self.v_ref, self.z_ref, self.o_ref, self.st_ref = q_ref, v_ref, z_ref, o_ref, st_ref
        self.rows, self.h, self.tri, self.masks, self.forward = rows, h, tri, masks, forward
        self.hs = slice(h * HEAD_DIM, (h + 1) * HEAD_DIM)


def _stage_gates(u, lb):
    lb = lb[:, u.hs]
    rest = 1.0 - lb
    z = u.z_ref[0, u.rows, u.hs]
    sig = 1.0 / (1.0 + jnp.exp(-z))
    gate = rest * sig
    g = jnp.log(jnp.maximum(lb + gate, F_FLOOR)) * (1.0 / np.log(2.0))
    u.k = rest - gate
    u.q = u.q_ref[0, u.rows, u.hs].astype(F32)
    u.v16 = u.v_ref[0, u.rows, u.hs]
    u.cs = _dot(u.tri, jnp.concatenate(_bf16_parts(g, CUMSUM_PARTS), axis=-1))


def _stage_levels(u):
    c = u.cs[:, :HEAD_DIM]
    for i in range(1, CUMSUM_PARTS):
        c = c + u.cs[:, i * HEAD_DIM:(i + 1) * HEAD_DIM]
    last = SCAN_CHUNK - 1 if u.forward else 0
    c_last = c[last:last + 1]
    u.q_in = (u.q * jnp.exp2(c)).astype(BF16)
    u.k_st = (u.k * jnp.exp2(c_last - c)).astype(BF16)
    u.e_last = jnp.exp2(c_last)
    u.levels = [_pair_level(u.q, u.k, c, b, u.forward) for b in _scan_levels()]
    u.levels.append(_diag_level(u.q, u.k, c, u.forward))


def _stage_scores(u):
    a = None
    for li, (qd, kd) in enumerate(u.levels):
        a_l = lax.dot_general(qd, kd, (((1,), (1,)), ((), ())), preferred_element_type=F32)
        a = a_l if li == 0 else jnp.where(u.masks[li], a_l, a)
    u.a16 = a.astype(BF16)


def _stage_output(u):
    contract_lanes = (((1,), (1,)), ((), ()))
    contract_rows = (((0,), (0,)), ((), ()))
    st = u.st_ref[u.h]
    o = _dot(u.a16, u.v16)
    o = o + lax.dot_general(u.q_in, st.astype(BF16), contract_lanes, preferred_element_type=F32)
    u.o_ref[0, u.rows, u.hs] = o
    u.st_ref[u.h] = st * u.e_last + lax.dot_general(u.v16, u.k_st, contract_rows,
                                                    preferred_element_type=F32)


def _scan_units(fwd_refs, bwd_refs, tri_f, tri_b):
    masks_f, masks_b = _level_masks(True), _level_masks(False)
    n_chunks = SCAN_BLOCK // SCAN_CHUNK
    units = []
    for ci in range(n_chunks):
        rows_f = slice(ci * SCAN_CHUNK, (ci + 1) * SCAN_CHUNK)
        cb = n_chunks - 1 - ci
        rows_b = slice(cb * SCAN_CHUNK, (cb + 1) * SCAN_CHUNK)
        for h in range(HEADS):
            units.append(_Unit(*fwd_refs, rows_f, h, tri_f, masks_f, True))
        for h in range(HEADS):
            units.append(_Unit(*bwd_refs, rows_b, h, tri_b, masks_b, False))
    return units


def _scan_kernel(lb_ref, mf_ref, mb_ref, qf_ref, vf_ref, zf_ref, qb_ref, vb_ref, zb_ref,
                 of_ref, ob_ref, sf_ref, sb_ref, *, layer):
    @pl.when(pl.program_id(1) == 0)
    def _():
        sf_ref[...] = jnp.zeros_like(sf_ref)
        sb_ref[...] = jnp.zeros_like(sb_ref)

    raw = lb_ref[...]
    ex = jnp.exp(raw - jnp.max(raw, axis=0, keepdims=True))
    soft = ex / jnp.sum(ex, axis=0, keepdims=True)
    lb = jnp.zeros((1, D_MODEL), F32)
    for i in range(1, layer + 1):
        lb = lb + soft[i:i + 1]

    units = _scan_units((qf_ref, vf_ref, zf_ref, of_ref, sf_ref), (qb_ref, vb_ref, zb_ref, ob_ref, sb_ref),
                        mf_ref[...], mb_ref[...])
    stages = (functools.partial(_stage_gates, lb=lb), _stage_levels, _stage_scores, _stage_output)
    _emit_pipeline(len(units), [(off, lambda n, fn=fn: fn(units[n])) for off, fn in zip(SCAN_SKEW, stages)])


def _scan(q, v, zf, zb, hgrn_lb, layer):
    bsz, s, _ = q.shape
    c = SCAN_BLOCK
    n_steps = s // c
    n_ctx = CTX_LEN // c
    mf = jnp.asarray(_cumsum_matrix(True), BF16)
    mb = jnp.asarray(_cumsum_matrix(False), BF16)

    def fwd_block(i):
        return jnp.where(i < n_ctx, n_steps - n_ctx + i, i - n_ctx)

    def bwd_block(i):
        return n_steps - 1 - i

    def rows(block_of):
        return pl.BlockSpec((1, c, D_MODEL), lambda b, i: (b, block_of(i), 0))

    fwd, bwd = rows(fwd_block), rows(bwd_block)
    out_sds = jax.ShapeDtypeStruct((bsz, s, D_MODEL), F32)
    return pl.pallas_call(
        functools.partial(_scan_kernel, layer=layer),
        grid=(bsz, n_steps),
        in_specs=[_resident(hgrn_lb), _resident(mf), _resident(mb), fwd, fwd, fwd, bwd, bwd, bwd],
        out_specs=[fwd, bwd],
        out_shape=[out_sds, out_sds],
        scratch_shapes=[pltpu.VMEM((HEADS, HEAD_DIM, HEAD_DIM), F32),
                        pltpu.VMEM((HEADS, HEAD_DIM, HEAD_DIM), F32)],
        compiler_params=_params(2),
        name="hgrn_scan",
    )(hgrn_lb, mf, mb, q, v, zf, q, v, zb)


def _readout_kernel(x_ref, of_ref, ob_ref, g_ref, mod_ref, nw_ref, gn_ref, w_ref, o_ref, *, n_tiles):
    b, i = pl.program_id(0), pl.program_id(1)
    blocks = _sub_blocks(i, n_tiles)
    ys = [None] * len(blocks)

    def mix(n):
        rows, _ = blocks[n]
        o = of_ref[0, rows] + ob_ref[0, rows]
        gn = gn_ref[...]
        heads = [_rms(o[:, h * HEAD_DIM:(h + 1) * HEAD_DIM]) * gn for h in range(HEADS)]
        y = jnp.concatenate(heads, axis=-1) * _silu(g_ref[0, rows].astype(F32))
        ys[n] = _dot(y.astype(BF16), w_ref[...])

    def finish(n):
        rows, is_ctx = blocks[n]
        o_ref[0, rows] = _postnorm_residual(x_ref[0, rows], ys[n], nw_ref[1:2, :], mod_ref, b, is_ctx, 2)

    _emit_pipeline(len(blocks), [(0, mix), (1, finish)])


def _readout(xs, o_f, o_b, g, mod_all, norm_w, gnorm_all, w_all, layer, j):
    bsz, s, d = xs.shape
    n_tiles = s // ROW_TILE
    tile = pl.BlockSpec((1, ROW_TILE, d), lambda b, i: (b, i, 0))
    return pl.pallas_call(
        functools.partial(_readout_kernel, n_tiles=n_tiles),
        grid=(bsz, n_tiles),
        in_specs=[tile, tile, tile, tile, _resident(mod_all, layer), _resident(norm_w, layer),
                  _resident(gnorm_all, j), _resident(w_all, j)],
        out_specs=tile,
        out_shape=jax.ShapeDtypeStruct(xs.shape, F32),
        compiler_params=_params(2),
        name="hgrn_readout",
    )(xs, o_f, o_b, g, mod_all, norm_w, gnorm_all, w_all)


def _conv_kernel(x_ref, mod_ref, nw_ref, wi_ref, cw_ref, wo_ref, o_ref, *, n_tiles, seq):
    b, i = pl.program_id(0), pl.program_id(1)
    d = D_MODEL
    blocks = _sub_blocks(i, n_tiles)
    hs, us, gs, ys = ([None] * len(blocks) for _ in range(4))

    def pre(n):
        rows, is_ctx = blocks[n]
        hs[n] = _prenorm(x_ref[0, rows], nw_ref[0:1, :], mod_ref, b, is_ctx, 0).astype(BF16)

    def project(n):
        gs[n] = _dot(hs[n], wi_ref[:, 0:d])
        us[n] = _dot(hs[n], wi_ref[:, d:2 * d]) * _dot(hs[n], wi_ref[:, 2 * d:3 * d])

    def convolve(n):
        rows, _ = blocks[n]
        u = us[n]
        pos = i * ROW_TILE + rows.start + lax.broadcasted_iota(jnp.int32, (SUB_ROWS, 1), 0)
        in_ctx = pos >= seq
        has_prev = (pos != seq) & (in_ctx | (pos % GRID_W != 0))
        has_next = (pos != seq + CTX_LEN - 1) & (in_ctx | (pos % GRID_W != GRID_W - 1))
        u_prev = jnp.where(has_prev, pltpu.roll(u, 1, axis=0), 0.0)
        u_next = jnp.where(has_next, pltpu.roll(u, SUB_ROWS - 1, axis=0), 0.0)
        y = u_prev * cw_ref[0:1, :] + u * cw_ref[1:2, :] + u_next * cw_ref[2:3, :]
        ys[n] = _dot((gs[n] * y).astype(BF16), wo_ref[...])

    def finish(n):
        rows, is_ctx = blocks[n]
        o_ref[0, rows] = _postnorm_residual(x_ref[0, rows], ys[n], nw_ref[1:2, :], mod_ref, b, is_ctx, 2)

    _emit_pipeline(len(blocks), [(0, pre), (1, project), (2, convolve), (3, finish)])


def _conv_mixer(xs, mod_all, norm_w, w_in_all, conv_w_all, w_out_all, layer, j):
    bsz, s, d = xs.shape
    n_tiles = s // ROW_TILE
    tile = pl.BlockSpec((1, ROW_TILE, d), lambda b, i: (b, i, 0))
    return pl.pallas_call(
        functools.partial(_conv_kernel, n_tiles=n_tiles, seq=s - CTX_LEN),
        grid=(bsz, n_tiles),
        in_specs=[tile, _resident(mod_all, layer), _resident(norm_w, layer), _resident(w_in_all, j),
                  _resident(conv_w_all, j), _resident(w_out_all, j)],
        out_specs=tile,
        out_shape=jax.ShapeDtypeStruct(xs.shape, F32),
        compiler_params=_params(2),
        name="conv_mixer",
    )(xs, mod_all, norm_w, w_in_all, conv_w_all, w_out_all)


def _ffn_kernel(x_ref, mod_ref, nw_ref, wi_ref, wo_ref, o_ref, *, n_tiles):
    b, i = pl.program_id(0), pl.program_id(1)
    hidden = wo_ref.shape[0]
    tf = hidden // FFN_SPLIT
    blocks = _sub_blocks(i, n_tiles)
    n_items = len(blocks) * FFN_SPLIT
    hs, accs = [None] * len(blocks), [None] * len(blocks)
    gates, ups, acts = ([None] * n_items for _ in range(3))

    def pre(n):
        r, f = divmod(n, FFN_SPLIT)
        if f == 0:
            rows, is_ctx = blocks[r]
            hs[r] = _prenorm(x_ref[0, rows], nw_ref[2:3, :], mod_ref, b, is_ctx, 3).astype(BF16)

    def project(n):
        r, f = divmod(n, FFN_SPLIT)
        gates[n] = _dot(hs[r], wi_ref[:, f * tf:(f + 1) * tf])
        ups[n] = _dot(hs[r], wi_ref[:, hidden + f * tf:hidden + (f + 1) * tf])

    def activate(n):
        acts[n] = (_silu(gates[n]) * ups[n]).astype(BF16)
        gates[n] = ups[n] = None

    def down(n):
        r, f = divmod(n, FFN_SPLIT)
        y = _dot(acts[n], wo_ref[f * tf:(f + 1) * tf, :])
        accs[r] = y if f == 0 else accs[r] + y
        acts[n] = None

    def finish(n):
        r, f = divmod(n, FFN_SPLIT)
        if f == FFN_SPLIT - 1:
            rows, is_ctx = blocks[r]
            o_ref[0, rows] = _postnorm_residual(x_ref[0, rows], accs[r], nw_ref[3:4, :],
                                                mod_ref, b, is_ctx, 5)

    _emit_pipeline(n_items, [(0, pre), (1, project), (2, activate), (3, down), (4, finish)])


def _ffn(xs, mod_all, norm_w, w_in_all, w_out_all, layer, out_rows):
    bsz, s, d = xs.shape
    n_tiles = s // ROW_TILE
    tile = pl.BlockSpec((1, ROW_TILE, d), lambda b, i: (b, i, 0))
    return pl.pallas_call(
        functools.partial(_ffn_kernel, n_tiles=n_tiles),
        grid=(bsz, n_tiles),
        in_specs=[tile, _resident(mod_all, layer), _resident(norm_w, layer),
                  _resident(w_in_all, layer), _resident(w_out_all, layer)],
        out_specs=tile,
        out_shape=jax.ShapeDtypeStruct((bsz, out_rows, d), F32),
        compiler_params=_params(2),
        name="swiglu_ffn",
    )(xs, mod_all, norm_w, w_in_all, w_out_all)


def kernel(x, c, ctx, c_ctx, ada_w, ada_b, norm_w, hgrn_w_in, hgrn_w_out, hgrn_gnorm, hgrn_lb,
           conv_w_in, conv_w, conv_w_out, ffn_w_in, ffn_w_out):
    bsz, seq, d = x.shape
    depth = ada_w.shape[0]
    s_all = seq + CTX_LEN
    assert d == D_MODEL and ctx.shape[1] == CTX_LEN and bsz < MOD_ROWS and depth % 2 == 0
    assert s_all % ROW_TILE == 0 and ROW_TILE % SUB_ROWS == 0 and SUB_ROWS % GRID_W == 0
    assert seq % SCAN_BLOCK == 0 and CTX_LEN % SCAN_BLOCK == 0 and seq % GRID_W == 0

    c_rows = jnp.zeros((MOD_ROWS, d), F32).at[:bsz].set(c).at[MOD_ROWS - 1].set(c_ctx)
    mod_all = _ada_table(c_rows, ada_w, ada_b)

    hgrn_w_in, hgrn_w_out, conv_w_in, conv_w_out, ffn_w_in, ffn_w_out = (
        w.astype(BF16) for w in (hgrn_w_in, hgrn_w_out, conv_w_in, conv_w_out, ffn_w_in, ffn_w_out))
    gnorm = hgrn_gnorm.reshape(-1, 1, HEAD_DIM)

    xs = None
    for l in range(depth):
        j = l // 2
        if l % 2 == 0:
            first = xs is None
            (q, v, zf, zb, g), xs = _in_proj(x if first else xs, ctx if first else None,
                                             mod_all, norm_w, hgrn_w_in, l, j)
            o_f, o_b = _scan(q, v, zf, zb, hgrn_lb, j)
            xs = _readout(xs, o_f, o_b, g, mod_all, norm_w, gnorm, hgrn_w_out, l, j)
        else:
            xs = _conv_mixer(xs, mod_all, norm_w, conv_w_in, conv_w, conv_w_out, l, j)
        out_rows = seq if l == depth - 1 else s_all
        xs = _ffn(xs, mod_all, norm_w, ffn_w_in, ffn_w_out, l, out_rows)
    return xs
```

```python
import functools

import numpy as np
import jax
import jax.numpy as jnp
from jax import lax
from jax.experimental import pallas as pl
from jax.experimental.pallas import tpu as pltpu

D_MODEL = 1024
CTX_LEN = 256
GRID_W = 64
HEADS = 8
HEAD_DIM = D_MODEL // HEADS
CONV_WIDTH = 3
N_MOD = 6
EPS = 1e-6
F_FLOOR = 1e-6

SCAN_BLOCK = 256
SCAN_CHUNK = 128
SCAN_BASE = 8
SCAN_SKEW = (0, 1, 2, 3)
CUMSUM_PARTS = 2
ROW_TILE = 768
SUB_ROWS = CTX_LEN
FFN_SPLIT = 2
MOD_ROWS = 8
VMEM_LIMIT = 56 * 1024 * 1024

F32 = jnp.float32
BF16 = jnp.bfloat16


def _bf16_parts(x, n):
    parts = []
    r = x
    for i in range(n):
        p = r.astype(BF16)
        parts.append(p)
        if i + 1 < n:
            r = r - p.astype(F32)
    return parts


def _dot(a, b):
    return jnp.dot(a, b, preferred_element_type=F32)


def _rms(x):
    return x * lax.rsqrt(jnp.mean(x * x, axis=-1, keepdims=True) + EPS)


def _silu(x):
    return x * (1.0 / (1.0 + jnp.exp(-x)))


def _emit_pipeline(n_items, stages):
    depth = max(off for off, _ in stages)
    for t in range(n_items + depth):
        for off, fn in stages:
            if 0 <= t - off < n_items:
                fn(t - off)


def _sub_blocks(tile_idx, n_tiles):
    n = ROW_TILE // SUB_ROWS
    blocks = []
    for r in range(n):
        rows = slice(r * SUB_ROWS, (r + 1) * SUB_ROWS)
        blocks.append((rows, (tile_idx == n_tiles - 1) if r == n - 1 else False))
    return blocks


def _mod_vec(mod_ref, batch, is_ctx, idx):
    row = batch if is_ctx is False else jnp.where(is_ctx, MOD_ROWS - 1, batch)
    return mod_ref[pl.ds(row, 1), idx * D_MODEL:(idx + 1) * D_MODEL]


def _prenorm(x, nw, mod_ref, batch, is_ctx, shift_idx):
    shift = _mod_vec(mod_ref, batch, is_ctx, shift_idx)
    scale = _mod_vec(mod_ref, batch, is_ctx, shift_idx + 1)
    return _rms(x) * (nw * (1.0 + scale)) + shift


def _postnorm_residual(x, y, nw, mod_ref, batch, is_ctx, gate_idx):
    gate = _mod_vec(mod_ref, batch, is_ctx, gate_idx)
    return x + _rms(y) * (gate * nw)


def _params(n_axes):
    return pltpu.CompilerParams(dimension_semantics=("arbitrary",) * n_axes,
                                vmem_limit_bytes=VMEM_LIMIT)


def _resident(a, layer=None):
    if layer is None:
        shape, index = a.shape, (0,) * a.ndim
    else:
        shape, index = (None,) + a.shape[1:], (layer,) + (0,) * (a.ndim - 1)
    return pl.BlockSpec(shape, lambda *_: index, pipeline_mode=pl.Buffered(1))


def _ada_kernel(c_ref, w_ref, b_ref, o_ref):
    a_hi, a_lo = _bf16_parts(_silu(c_ref[...]), 2)
    w_hi, w_lo = _bf16_parts(w_ref[0], 2)
    o_ref[0] = _dot(a_hi, w_hi) + _dot(a_lo, w_hi) + _dot(a_hi, w_lo) + b_ref[0]


def _ada_table(c_rows, ada_w, ada_b):
    depth, d, n = ada_w.shape
    tn = 1536
    return pl.pallas_call(
        _ada_kernel,
        grid=(depth, n // tn),
        in_specs=[
            pl.BlockSpec((MOD_ROWS, d), lambda l, j: (0, 0)),
            pl.BlockSpec((1, d, tn), lambda l, j: (l, 0, j)),
            pl.BlockSpec((1, 1, tn), lambda l, j: (l, 0, j)),
        ],
        out_specs=pl.BlockSpec((1, MOD_ROWS, tn), lambda l, j: (l, 0, j)),
        out_shape=jax.ShapeDtypeStruct((depth, MOD_ROWS, n), F32),
        compiler_params=_params(2),
        name="ada_table",
    )(c_rows, ada_w, ada_b.reshape(depth, 1, n))


def _in_proj_kernel(*refs, joins, n_tiles):
    n_blk = ROW_TILE // SUB_ROWS
    x_refs, refs = refs[:n_blk], refs[n_blk:]
    ctx_ref = refs[0] if joins else None
    mod_ref, nw_ref, w_ref, q_ref, v_ref, zf_ref, zb_ref, g_ref = refs[joins:joins + 8]
    xs_ref = refs[-1] if joins else None
    b, i = pl.program_id(0), pl.program_id(1)
    d = D_MODEL
    blocks = _sub_blocks(i, n_tiles)
    hs = [None] * n_blk

    def pre(n):
        rows, is_ctx = blocks[n]
        x = x_refs[n][0]
        if joins:
            if is_ctx is not False:
                x = jnp.where(is_ctx, ctx_ref[0], x)
            xs_ref[0, rows] = x
        hs[n] = _prenorm(x, nw_ref[0:1, :], mod_ref, b, is_ctx, 0).astype(BF16)

    def project(n):
        rows, _ = blocks[n]
        h = hs[n]
        q_ref[0, rows] = (_dot(h, w_ref[:, 0:d]) * (HEAD_DIM ** -0.5)).astype(BF16)
        v_ref[0, rows] = _dot(h, w_ref[:, d:2 * d]).astype(BF16)
        zf_ref[0, rows] = _dot(h, w_ref[:, 2 * d:3 * d])
        zb_ref[0, rows] = _dot(h, w_ref[:, 3 * d:4 * d])
        g_ref[0, rows] = _dot(h, w_ref[:, 4 * d:5 * d]).astype(BF16)

    _emit_pipeline(n_blk, [(0, pre), (1, project)])


def _in_proj(x, ctx, mod_all, norm_w, w_all, layer, j):
    bsz, _, d = x.shape
    joins = ctx is not None
    s = x.shape[1] + (CTX_LEN if joins else 0)
    n_tiles = s // ROW_TILE
    n_blk = ROW_TILE // SUB_ROWS
    last_block = x.shape[1] // SUB_ROWS - 1
    tile = pl.BlockSpec((1, ROW_TILE, d), lambda b, i: (b, i, 0))

    def x_block(n):
        return pl.BlockSpec((1, SUB_ROWS, d), lambda b, i: (b, jnp.minimum(i * n_blk + n, last_block), 0))

    in_specs = [x_block(n) for n in range(n_blk)]
    in_specs += [pl.BlockSpec((1, CTX_LEN, d), lambda b, i: (b, 0, 0))] if joins else []
    in_specs += [_resident(mod_all, layer), _resident(norm_w, layer), _resident(w_all, j)]
    sds16 = jax.ShapeDtypeStruct((bsz, s, d), BF16)
    sds32 = jax.ShapeDtypeStruct((bsz, s, d), F32)
    out = pl.pallas_call(
        functools.partial(_in_proj_kernel, joins=joins, n_tiles=n_tiles),
        grid=(bsz, n_tiles),
        in_specs=in_specs,
        out_specs=[tile] * (6 if joins else 5),
        out_shape=[sds16, sds16, sds32, sds32, sds16] + ([sds32] if joins else []),
        compiler_params=_params(2),
        name="hgrn_in_proj",
    )(*([x] * n_blk), *([ctx] if joins else []), mod_all, norm_w, w_all)
    return (out[:5], out[5]) if joins else (out, x)


def _scan_levels():
    levels = []
    b = SCAN_CHUNK // 2
    while b >= SCAN_BASE:
        levels.append(b)
        b //= 2
    return levels


def _cumsum_matrix(forward):
    t = np.arange(SCAN_CHUNK)
    tri = t[None, :] <= t[:, None] if forward else t[None, :] >= t[:, None]
    return tri.astype(np.float32)


def _level_masks(forward):
    c = SCAN_CHUNK
    t = lax.broadcasted_iota(jnp.int32, (c, c), 0)
    s = lax.broadcasted_iota(jnp.int32, (c, c), 1)
    if not forward:
        t, s = (c - 1) - t, (c - 1) - s
    masks = []
    for b in _scan_levels():
        same = (t // (2 * b)) == (s // (2 * b))
        masks.append(same & (t % (2 * b) >= b) & (s % (2 * b) < b))
    same = (t // SCAN_BASE) == (s // SCAN_BASE)
    masks.append(same & (s <= t))
    return masks


def _pair_level(q, k, c, half, forward):
    packed_rows = 16
    qs, ks = [], []
    for start in range(0, SCAN_CHUNK, 2 * half):
        lo, hi = slice(start, start + half), slice(start + half, start + 2 * half)
        if forward:
            ref, q_rows, k_rows = c[start + half - 1:start + half], hi, lo
        else:
            ref, q_rows, k_rows = c[start + half:start + half + 1], lo, hi
        eq = jnp.exp2(c[q_rows] - ref)
        ek = jnp.exp2(ref - c[k_rows])
        if half % packed_rows == 0:
            zero = jnp.zeros((half, c.shape[-1]), BF16)
            qe, ke = (q[q_rows] * eq).astype(BF16), (k[k_rows] * ek).astype(BF16)
            qs += [zero, qe] if forward else [qe, zero]
            ks += [ke, zero] if forward else [zero, ke]
        else:
            e = jnp.concatenate([ek, eq] if forward else [eq, ek], axis=0)
            rows = slice(start, start + 2 * half)
            qs.append((q[rows] * e).astype(BF16))
            ks.append((k[rows] * e).astype(BF16))
    return jnp.concatenate(qs, axis=0), jnp.concatenate(ks, axis=0)


def _diag_level(q, k, c, forward):
    c3 = c.reshape(SCAN_CHUNK // SCAN_BASE, SCAN_BASE, c.shape[-1])
    r = SCAN_BASE // 2 - 1 if forward else SCAN_BASE // 2
    x = (c3 - c3[:, r:r + 1, :]).reshape(c.shape)
    return (q * jnp.exp2(x)).astype(BF16), (k * jnp.exp2(-x)).astype(BF16)


class _Unit:
    def __init__(self, q_ref, v_ref, z_ref, o_ref, st_ref, rows, h, tri, masks, forward):
        self.q_ref, self.v_ref, self.z_ref, self.o_ref, self.st_ref = q_ref, v_ref, z_ref, o_ref, st_ref
        self.rows, self.h, self.tri, self.masks, self.forward = rows, h, tri, masks, forward
        self.hs = slice(h * HEAD_DIM, (h + 1) * HEAD_DIM)


def _stage_gates(u, lb):
    lb = lb[:, u.hs]
    rest = 1.0 - lb
    z = u.z_ref[0, u.rows, u.hs]
    sig = 1.0 / (1.0 + jnp.exp(-z))
    gate = rest * sig
    g = jnp.log(jnp.maximum(lb + gate, F_FLOOR)) * (1.0 / np.log(2.0))
    u.k = rest - gate
    u.q = u.q_ref[0, u.rows, u.hs].astype(F32)
    u.v16 = u.v_ref[0, u.rows, u.hs]
    u.cs = _dot(u.tri, jnp.concatenate(_bf16_parts(g, CUMSUM_PARTS), axis=-1))


def _stage_levels(u):
    c = u.cs[:, :HEAD_DIM]
    for i in range(1, CUMSUM_PARTS):
        c = c + u.cs[:, i * HEAD_DIM:(i + 1) * HEAD_DIM]
    last = SCAN_CHUNK - 1 if u.forward else 0
    c_last = c[last:last + 1]
    u.q_in = (u.q * jnp.exp2(c)).astype(BF16)
    u.k_st = (u.k * jnp.exp2(c_last - c)).astype(BF16)
    u.e_last = jnp.exp2(c_last)
    u.levels = [_pair_level(u.q, u.k, c, b, u.forward) for b in _scan_levels()]
    u.levels.append(_diag_level(u.q, u.k, c, u.forward))


def _stage_scores(u):
    a = None
    for li, (qd, kd) in enumerate(u.levels):
        a_l = lax.dot_general(qd, kd, (((1,), (1,)), ((), ())), preferred_element_type=F32)
        a = a_l if li == 0 else jnp.where(u.masks[li], a_l, a)
    u.a16 = a.astype(BF16)


def _stage_output(u):
    contract_lanes = (((1,), (1,)), ((), ()))
    contract_rows = (((0,), (0,)), ((), ()))
    st = u.st_ref[u.h]
    o = _dot(u.a16, u.v16)
    o = o + lax.dot_general(u.q_in, st.astype(BF16), contract_lanes, preferred_element_type=F32)
    u.o_ref[0, u.rows, u.hs] = o.astype(BF16)
    u.st_ref[u.h] = st * u.e_last + lax.dot_general(u.v16, u.k_st, contract_rows,
                                                    preferred_element_type=F32)


def _scan_units(fwd_refs, bwd_refs, tri_f, tri_b):
    masks_f, masks_b = _level_masks(True), _level_masks(False)
    n_chunks = SCAN_BLOCK // SCAN_CHUNK
    units = []
    for ci in range(n_chunks):
        rows_f = slice(ci * SCAN_CHUNK, (ci + 1) * SCAN_CHUNK)
        cb = n_chunks - 1 - ci
        rows_b = slice(cb * SCAN_CHUNK, (cb + 1) * SCAN_CHUNK)
        for h in range(HEADS):
            units.append(_Unit(*fwd_refs, rows_f, h, tri_f, masks_f, True))
        for h in range(HEADS):
            units.append(_Unit(*bwd_refs, rows_b, h, tri_b, masks_b, False))
    return units


def _scan_kernel(lb_ref, mf_ref, mb_ref, qf_ref, vf_ref, zf_ref, qb_ref, vb_ref, zb_ref,
                 of_ref, ob_ref, sf_ref, sb_ref, *, layer):
    @pl.when(pl.program_id(1) == 0)
    def _():
        sf_ref[...] = jnp.zeros_like(sf_ref)
        sb_ref[...] = jnp.zeros_like(sb_ref)

    raw = lb_ref[...]
    ex = jnp.exp(raw - jnp.max(raw, axis=0, keepdims=True))
    soft = ex / jnp.sum(ex, axis=0, keepdims=True)
    lb = jnp.zeros((1, D_MODEL), F32)
    for i in range(1, layer + 1):
        lb = lb + soft[i:i + 1]

    units = _scan_units((qf_ref, vf_ref, zf_ref, of_ref, sf_ref), (qb_ref, vb_ref, zb_ref, ob_ref, sb_ref),
                        mf_ref[...], mb_ref[...])
    stages = (functools.partial(_stage_gates, lb=lb), _stage_levels, _stage_scores, _stage_output)
    _emit_pipeline(len(units), [(off, lambda n, fn=fn: fn(units[n])) for off, fn in zip(SCAN_SKEW, stages)])


def _scan(q, v, zf, zb, hgrn_lb, layer):
    bsz, s, _ = q.shape
    c = SCAN_BLOCK
    n_steps = s // c
    n_ctx = CTX_LEN // c
    mf = jnp.asarray(_cumsum_matrix(True), BF16)
    mb = jnp.asarray(_cumsum_matrix(False), BF16)

    def fwd_block(i):
        return jnp.where(i < n_ctx, n_steps - n_ctx + i, i - n_ctx)

    def bwd_block(i):
        return n_steps - 1 - i

    def rows(block_of):
        return pl.BlockSpec((1, c, D_MODEL), lambda b, i: (b, block_of(i), 0))

    fwd, bwd = rows(fwd_block), rows(bwd_block)
    out_sds = jax.ShapeDtypeStruct((bsz, s, D_MODEL), BF16)
    return pl.pallas_call(
        functools.partial(_scan_kernel, layer=layer),
        grid=(bsz, n_steps),
        in_specs=[_resident(hgrn_lb), _resident(mf), _resident(mb), fwd, fwd, fwd, bwd, bwd, bwd],
        out_specs=[fwd, bwd],
        out_shape=[out_sds, out_sds],
        scratch_shapes=[pltpu.VMEM((HEADS, HEAD_DIM, HEAD_DIM), F32),
                        pltpu.VMEM((HEADS, HEAD_DIM, HEAD_DIM), F32)],
        compiler_params=_params(2),
        name="hgrn_scan",
    )(hgrn_lb, mf, mb, q, v, zf, q, v, zb)


def _readout_ffn_kernel(x_ref, of_ref, ob_ref, g_ref, mod_ref, nw_ref, gn_ref, w_ref, wi_ref, wo_ref,
                        o_ref, *, n_tiles):
    b, i = pl.program_id(0), pl.program_id(1)
    tail = _FfnTail(x_ref, o_ref, mod_ref, nw_ref, wi_ref, wo_ref, b, _sub_blocks(i, n_tiles))
    mixed = [None] * len(tail.blocks)

    def mix(r):
        rows, _ = tail.blocks[r]
        o = of_ref[0, rows].astype(F32) + ob_ref[0, rows].astype(F32)
        gn = gn_ref[...]
        heads = [_rms(o[:, h * HEAD_DIM:(h + 1) * HEAD_DIM]) * gn for h in range(HEADS)]
        y = jnp.concatenate(heads, axis=-1) * _silu(g_ref[0, rows].astype(F32))
        mixed[r] = y.astype(BF16)

    def project_out(r):
        tail.ys[r] = _dot(mixed[r], w_ref[...])

    tail.emit([mix, project_out])


def _readout_ffn(xs, o_f, o_b, g, mod_all, norm_w, gnorm_all, w_out_all, ffn_w_in, ffn_w_out,
                 layer, j, out_rows):
    bsz, s, d = xs.shape
    n_tiles = s // ROW_TILE
    tile = pl.BlockSpec((1, ROW_TILE, d), lambda b, i: (b, i, 0))
    return pl.pallas_call(
        functools.partial(_readout_ffn_kernel, n_tiles=n_tiles),
        grid=(bsz, n_tiles),
        in_specs=[tile, tile, tile, tile, _resident(mod_all, layer), _resident(norm_w, layer),
                  _resident(gnorm_all, j), _resident(w_out_all, j),
                  _resident(ffn_w_in, layer), _resident(ffn_w_out, layer)],
        out_specs=tile,
        out_shape=jax.ShapeDtypeStruct((bsz, out_rows, d), F32),
        compiler_params=_params(2),
        name="hgrn_readout_ffn",
    )(xs, o_f, o_b, g, mod_all, norm_w, gnorm_all, w_out_all, ffn_w_in, ffn_w_out)


def _conv_ffn_kernel(x_ref, mod_ref, nw_ref, ci_ref, cw_ref, co_ref, wi_ref, wo_ref, o_ref,
                     *, n_tiles, seq):
    b, i = pl.program_id(0), pl.program_id(1)
    d = D_MODEL
    tail = _FfnTail(x_ref, o_ref, mod_ref, nw_ref, wi_ref, wo_ref, b, _sub_blocks(i, n_tiles))
    n_blk = len(tail.blocks)
    hs, us, gs, ts = ([None] * n_blk for _ in range(4))

    def pre(r):
        rows, is_ctx = tail.blocks[r]
        hs[r] = _prenorm(x_ref[0, rows], nw_ref[0:1, :], mod_ref, b, is_ctx, 0).astype(BF16)

    def project(r):
        gs[r] = _dot(hs[r], ci_ref[:, 0:d])
        us[r] = _dot(hs[r], ci_ref[:, d:2 * d]) * _dot(hs[r], ci_ref[:, 2 * d:3 * d])

    def convolve(r):
        rows, _ = tail.blocks[r]
        u = us[r]
        pos = i * ROW_TILE + rows.start + lax.broadcasted_iota(jnp.int32, (SUB_ROWS, 1), 0)
        in_ctx = pos >= seq
        has_prev = (pos != seq) & (in_ctx | (pos % GRID_W != 0))
        has_next = (pos != seq + CTX_LEN - 1) & (in_ctx | (pos % GRID_W != GRID_W - 1))
        u_prev = jnp.where(has_prev, pltpu.roll(u, 1, axis=0), 0.0)
        u_next = jnp.where(has_next, pltpu.roll(u, SUB_ROWS - 1, axis=0), 0.0)
        y = u_prev * cw_ref[0:1, :] + u * cw_ref[1:2, :] + u_next * cw_ref[2:3, :]
        ts[r] = (gs[r] * y).astype(BF16)

    def project_out(r):
        tail.ys[r] = _dot(ts[r], co_ref[...])

    tail.emit([pre, project, convolve, project_out])


def _conv_ffn(xs, mod_all, norm_w, w_in_all, conv_w_all, w_out_all, ffn_w_in, ffn_w_out,
              layer, j, out_rows):
    bsz, s, d = xs.shape
    n_tiles = s // ROW_TILE
    tile = pl.BlockSpec((1, ROW_TILE, d), lambda b, i: (b, i, 0))
    return pl.pallas_call(
        functools.partial(_conv_ffn_kernel, n_tiles=n_tiles, seq=s - CTX_LEN),
        grid=(bsz, n_tiles),
        in_specs=[tile, _resident(mod_all, layer), _resident(norm_w, layer), _resident(w_in_all, j),
                  _resident(conv_w_all, j), _resident(w_out_all, j),
                  _resident(ffn_w_in, layer), _resident(ffn_w_out, layer)],
        out_specs=tile,
        out_shape=jax.ShapeDtypeStruct((bsz, out_rows, d), F32),
        compiler_params=_params(2),
        name="conv_ffn",
    )(xs, mod_all, norm_w, w_in_all, conv_w_all, w_out_all, ffn_w_in, ffn_w_out)


class _FfnTail:
    def __init__(self, x_ref, o_ref, mod_ref, nw_ref, wi_ref, wo_ref, batch, blocks):
        self.x_ref, self.o_ref, self.mod_ref, self.nw_ref = x_ref, o_ref, mod_ref, nw_ref
        self.wi_ref, self.wo_ref, self.batch, self.blocks = wi_ref, wo_ref, batch, blocks
        self.hidden = wo_ref.shape[0]
        self.tf = self.hidden // FFN_SPLIT
        n_blk = len(blocks)
        self.ys, self.x1, self.hs, self.accs = ([None] * n_blk for _ in range(4))
        self.gates, self.ups, self.acts = ([None] * (n_blk * FFN_SPLIT) for _ in range(3))

    def residual(self, r):
        rows, is_ctx = self.blocks[r]
        x1 = _postnorm_residual(self.x_ref[0, rows], self.ys[r], self.nw_ref[1:2, :],
                                self.mod_ref, self.batch, is_ctx, 2)
        self.x1[r] = x1
        self.hs[r] = _prenorm(x1, self.nw_ref[2:3, :], self.mod_ref, self.batch, is_ctx, 3).astype(BF16)

    def project(self, n):
        r, f = divmod(n, FFN_SPLIT)
        lo, hi = f * self.tf, (f + 1) * self.tf
        self.gates[n] = _dot(self.hs[r], self.wi_ref[:, lo:hi])
        self.ups[n] = _dot(self.hs[r], self.wi_ref[:, self.hidden + lo:self.hidden + hi])

    def activate(self, n):
        self.acts[n] = (_silu(self.gates[n]) * self.ups[n]).astype(BF16)
        self.gates[n] = self.ups[n] = None

    def down(self, n):
        r, f = divmod(n, FFN_SPLIT)
        y = _dot(self.acts[n], self.wo_ref[f * self.tf:(f + 1) * self.tf, :])
        self.accs[r] = y if f == 0 else self.accs[r] + y
        self.acts[n] = None

    def finish(self, r):
        rows, is_ctx = self.blocks[r]
        self.o_ref[0, rows] = _postnorm_residual(self.x1[r], self.accs[r], self.nw_ref[3:4, :],
                                                 self.mod_ref, self.batch, is_ctx, 5)

    def emit(self, mixer_stages):
        def on_first(fn):
            return lambda n: fn(n // FFN_SPLIT) if n % FFN_SPLIT == 0 else None

        def on_last(fn):
            return lambda n: fn(n // FFN_SPLIT) if n % FFN_SPLIT == FFN_SPLIT - 1 else None

        stages = [on_first(fn) for fn in mixer_stages]
        stages += [on_first(self.residual), self.project, self.activate, self.down, on_last(self.finish)]
        _emit_pipeline(len(self.blocks) * FFN_SPLIT, list(enumerate(stages)))


def kernel(x, c, ctx, c_ctx, ada_w, ada_b, norm_w, hgrn_w_in, hgrn_w_out, hgrn_gnorm, hgrn_lb,
           conv_w_in, conv_w, conv_w_out, ffn_w_in, ffn_w_out):
    bsz, seq, d = x.shape
    depth = ada_w.shape[0]
    s_all = seq + CTX_LEN
    assert d == D_MODEL and ctx.shape[1] == CTX_LEN and bsz < MOD_ROWS and depth % 2 == 0
    assert s_all % ROW_TILE == 0 and ROW_TILE % SUB_ROWS == 0 and SUB_ROWS % GRID_W == 0
    assert seq % SCAN_BLOCK == 0 and CTX_LEN % SCAN_BLOCK == 0 and seq % GRID_W == 0

    c_rows = jnp.zeros((MOD_ROWS, d), F32).at[:bsz].set(c).at[MOD_ROWS - 1].set(c_ctx)
    mod_all = _ada_table(c_rows, ada_w, ada_b)

    hgrn_w_in, hgrn_w_out, conv_w_in, conv_w_out, ffn_w_in, ffn_w_out = (
        w.astype(BF16) for w in (hgrn_w_in, hgrn_w_out, conv_w_in, conv_w_out, ffn_w_in, ffn_w_out))
    gnorm = hgrn_gnorm.reshape(-1, 1, HEAD_DIM)

    xs = None
    for l in range(depth):
        j = l // 2
        out_rows = seq if l == depth - 1 else s_all
        if l % 2 == 0:
            first = xs is None
            (q, v, zf, zb, g), xs = _in_proj(x if first else xs, ctx if first else None,
                                             mod_all, norm_w, hgrn_w_in, l, j)
            o_f, o_b = _scan(q, v, zf, zb, hgrn_lb, j)
            xs = _readout_ffn(xs, o_f, o_b, g, mod_all, norm_w, gnorm, hgrn_w_out,
                              ffn_w_in, ffn_w_out, l, j, out_rows)
        else:
            xs = _conv_ffn(xs, mod_all, norm_w, conv_w_in, conv_w, conv_w_out,
                           ffn_w_in, ffn_w_out, l, j, out_rows)
    return xs
```

```python
import functools

import numpy as np
import jax
import jax.numpy as jnp
from jax import lax
from jax.experimental import pallas as pl
from jax.experimental.pallas import tpu as pltpu

D_MODEL = 1024
CTX_LEN = 256
GRID_W = 64
HEADS = 8
HEAD_DIM = D_MODEL // HEADS
CONV_WIDTH = 3
N_MOD = 6
EPS = 1e-6
F_FLOOR = 1e-6

SCAN_BLOCK = 256
SCAN_CHUNK = 128
SCAN_BASE = 8
SCAN_SKEW = (0, 1, 2, 3)
KT_SLOTS = 4
CUMSUM_PARTS = 2
ROW_TILE = 768
SUB_ROWS = CTX_LEN
FFN_SPLIT = 2
MOD_ROWS = 8
VMEM_LIMIT = 56 * 1024 * 1024

F32 = jnp.float32
BF16 = jnp.bfloat16


def _bf16_parts(x, n):
    parts = []
    r = x
    for i in range(n):
        p = r.astype(BF16)
        parts.append(p)
        if i + 1 < n:
            r = r - p.astype(F32)
    return parts


def _dot(a, b):
    return jnp.dot(a, b, preferred_element_type=F32)


def _rms(x):
    return x * lax.rsqrt(jnp.mean(x * x, axis=-1, keepdims=True) + EPS)


def _silu(x):
    return x * (1.0 / (1.0 + jnp.exp(-x)))


def _emit_pipeline(n_items, stages):
    depth = max(off for off, _ in stages)
    for t in range(n_items + depth):
        for off, fn in stages:
            if 0 <= t - off < n_items:
                fn(t - off)


def _sub_blocks(tile_idx, n_tiles):
    n = ROW_TILE // SUB_ROWS
    blocks = []
    for r in range(n):
        rows = slice(r * SUB_ROWS, (r + 1) * SUB_ROWS)
        blocks.append((rows, (tile_idx == n_tiles - 1) if r == n - 1 else False))
    return blocks


def _mod_vec(mod_ref, batch, is_ctx, idx):
    row = batch if is_ctx is False else jnp.where(is_ctx, MOD_ROWS - 1, batch)
    return mod_ref[pl.ds(row, 1), idx * D_MODEL:(idx + 1) * D_MODEL]


def _prenorm(x, nw, mod_ref, batch, is_ctx, shift_idx):
    shift = _mod_vec(mod_ref, batch, is_ctx, shift_idx)
    scale = _mod_vec(mod_ref, batch, is_ctx, shift_idx + 1)
    return _rms(x) * (nw * (1.0 + scale)) + shift


def _postnorm_residual(x, y, nw, mod_ref, batch, is_ctx, gate_idx):
    gate = _mod_vec(mod_ref, batch, is_ctx, gate_idx)
    return x + _rms(y) * (gate * nw)


def _params(n_axes):
    return pltpu.CompilerParams(dimension_semantics=("arbitrary",) * n_axes,
                                vmem_limit_bytes=VMEM_LIMIT)


def _resident(a, layer=None):
    if layer is None:
        shape, index = a.shape, (0,) * a.ndim
    else:
        shape, index = (None,) + a.shape[1:], (layer,) + (0,) * (a.ndim - 1)
    return pl.BlockSpec(shape, lambda *_: index, pipeline_mode=pl.Buffered(1))


def _ada_kernel(c_ref, w_ref, b_ref, o_ref):
    a_hi, a_lo = _bf16_parts(_silu(c_ref[...]), 2)
    w_hi, w_lo = _bf16_parts(w_ref[0], 2)
    o_ref[0] = _dot(a_hi, w_hi) + _dot(a_lo, w_hi) + _dot(a_hi, w_lo) + b_ref[0]


def _ada_table(c_rows, ada_w, ada_b):
    depth, d, n = ada_w.shape
    tn = 1536
    return pl.pallas_call(
        _ada_kernel,
        grid=(depth, n // tn),
        in_specs=[
            pl.BlockSpec((MOD_ROWS, d), lambda l, j: (0, 0)),
            pl.BlockSpec((1, d, tn), lambda l, j: (l, 0, j)),
            pl.BlockSpec((1, 1, tn), lambda l, j: (l, 0, j)),
        ],
        out_specs=pl.BlockSpec((1, MOD_ROWS, tn), lambda l, j: (l, 0, j)),
        out_shape=jax.ShapeDtypeStruct((depth, MOD_ROWS, n), F32),
        compiler_params=_params(2),
        name="ada_table",
    )(c_rows, ada_w, ada_b.reshape(depth, 1, n))


def _in_proj_kernel(*refs, joins, n_tiles):
    n_blk = ROW_TILE // SUB_ROWS
    x_refs, refs = refs[:n_blk], refs[n_blk:]
    ctx_ref = refs[0] if joins else None
    mod_ref, nw_ref, w_ref, q_ref, v_ref, zf_ref, zb_ref, g_ref = refs[joins:joins + 8]
    xs_ref = refs[-1] if joins else None
    b, i = pl.program_id(0), pl.program_id(1)
    d = D_MODEL
    blocks = _sub_blocks(i, n_tiles)
    hs = [None] * n_blk

    def pre(n):
        rows, is_ctx = blocks[n]
        x = x_refs[n][0]
        if joins:
            if is_ctx is not False:
                x = jnp.where(is_ctx, ctx_ref[0], x)
            xs_ref[0, rows] = x
        hs[n] = _prenorm(x, nw_ref[0:1, :], mod_ref, b, is_ctx, 0).astype(BF16)

    def project(n):
        rows, _ = blocks[n]
        h = hs[n]
        q_ref[0, rows] = (_dot(h, w_ref[:, 0:d]) * (HEAD_DIM ** -0.5)).astype(BF16)
        v_ref[0, rows] = _dot(h, w_ref[:, d:2 * d]).astype(BF16)
        zf_ref[0, rows] = _dot(h, w_ref[:, 2 * d:3 * d])
        zb_ref[0, rows] = _dot(h, w_ref[:, 3 * d:4 * d])
        g_ref[0, rows] = _dot(h, w_ref[:, 4 * d:5 * d]).astype(BF16)

    _emit_pipeline(n_blk, [(0, pre), (1, project)])


def _in_proj(x, ctx, mod_all, norm_w, w, layer):
    bsz, _, d = x.shape
    joins = ctx is not None
    s = x.shape[1] + (CTX_LEN if joins else 0)
    n_tiles = s // ROW_TILE
    n_blk = ROW_TILE // SUB_ROWS
    last_block = x.shape[1] // SUB_ROWS - 1
    tile = pl.BlockSpec((1, ROW_TILE, d), lambda b, i: (b, i, 0))

    def x_block(n):
        return pl.BlockSpec((1, SUB_ROWS, d), lambda b, i: (b, jnp.minimum(i * n_blk + n, last_block), 0))

    in_specs = [x_block(n) for n in range(n_blk)]
    in_specs += [pl.BlockSpec((1, CTX_LEN, d), lambda b, i: (b, 0, 0))] if joins else []
    in_specs += [_resident(mod_all, layer), _resident(norm_w, layer), _resident(w)]
    sds16 = jax.ShapeDtypeStruct((bsz, s, d), BF16)
    sds32 = jax.ShapeDtypeStruct((bsz, s, d), F32)
    out = pl.pallas_call(
        functools.partial(_in_proj_kernel, joins=joins, n_tiles=n_tiles),
        grid=(bsz, n_tiles),
        in_specs=in_specs,
        out_specs=[tile] * (6 if joins else 5),
        out_shape=[sds16, sds16, sds32, sds32, sds16] + ([sds32] if joins else []),
        compiler_params=_params(2),
        name="hgrn_in_proj",
    )(*([x] * n_blk), *([ctx] if joins else []), mod_all, norm_w, w)
    return (out[:5], out[5]) if joins else (out, x)


def _scan_levels():
    levels = []
    b = SCAN_CHUNK // 2
    while b >= SCAN_BASE:
        levels.append(b)
        b //= 2
    return levels


def _cumsum_matrix(forward):
    t = np.arange(SCAN_CHUNK)
    tri = t[None, :] <= t[:, None] if forward else t[None, :] >= t[:, None]
    return tri.astype(np.float32)


def _level_masks(forward):
    c = SCAN_CHUNK
    t = lax.broadcasted_iota(jnp.int32, (c, c), 0)
    s = lax.broadcasted_iota(jnp.int32, (c, c), 1)
    if not forward:
        t, s = (c - 1) - t, (c - 1) - s
    masks = []
    for b in _scan_levels():
        same = (t // (2 * b)) == (s // (2 * b))
        masks.append(same & (t % (2 * b) >= b) & (s % (2 * b) < b))
    same = (t // SCAN_BASE) == (s // SCAN_BASE)
    masks.append(same & (s <= t))
    return masks


def _pair_level(q, k, c, half, forward):
    packed_rows = 16
    qs, ks = [], []
    for start in range(0, SCAN_CHUNK, 2 * half):
        lo, hi = slice(start, start + half), slice(start + half, start + 2 * half)
        if forward:
            ref, q_rows, k_rows = c[start + half - 1:start + half], hi, lo
        else:
            ref, q_rows, k_rows = c[start + half:start + half + 1], lo, hi
        eq = jnp.exp2(c[q_rows] - ref)
        ek = jnp.exp2(ref - c[k_rows])
        if half % packed_rows == 0:
            zero = jnp.zeros((half, c.shape[-1]), BF16)
            qe, ke = (q[q_rows] * eq).astype(BF16), (k[k_rows] * ek).astype(BF16)
            qs += [zero, qe] if forward else [qe, zero]
            ks += [ke, zero] if forward else [zero, ke]
        else:
            e = jnp.concatenate([ek, eq] if forward else [eq, ek], axis=0)
            rows = slice(start, start + 2 * half)
            qs.append((q[rows] * e).astype(BF16))
            ks.append((k[rows] * e).astype(BF16))
    return jnp.concatenate(qs, axis=0), jnp.concatenate(ks, axis=0)


def _diag_level(q, k, c, forward):
    c3 = c.reshape(SCAN_CHUNK // SCAN_BASE, SCAN_BASE, c.shape[-1])
    r = SCAN_BASE // 2 - 1 if forward else SCAN_BASE // 2
    x = (c3 - c3[:, r:r + 1, :]).reshape(c.shape)
    return (q * jnp.exp2(x)).astype(BF16), (k * jnp.exp2(-x)).astype(BF16)


class _Unit:
    def __init__(self, q_ref, v_ref, z_ref, o_ref, st_ref, kt_ref, slot, rows, h, tri, masks, forward):
        self.q_ref, self.v_ref, self.z_ref, self.o_ref, self.st_ref = q_ref, v_ref, z_ref, o_ref, st_ref
        self.kt_ref, self.slot = kt_ref, slot
        self.rows, self.h, self.tri, self.masks, self.forward = rows, h, tri, masks, forward
        self.hs = slice(h * HEAD_DIM, (h + 1) * HEAD_DIM)


def _stage_gates(u, lb):
    lb = lb[:, u.hs]
    rest = 1.0 - lb
    z = u.z_ref[0, u.rows, u.hs]
    sig = 1.0 / (1.0 + jnp.exp(-z))
    gate = rest * sig
    g = jnp.log(jnp.maximum(lb + gate, F_FLOOR)) * (1.0 / np.log(2.0))
    u.k = rest - gate
    u.q = u.q_ref[0, u.rows, u.hs].astype(F32)
    u.v16 = u.v_ref[0, u.rows, u.hs]
    u.cs = _dot(u.tri, jnp.concatenate(_bf16_parts(g, CUMSUM_PARTS), axis=-1))


def _stage_levels(u):
    c = u.cs[:, :HEAD_DIM]
    for i in range(1, CUMSUM_PARTS):
        c = c + u.cs[:, i * HEAD_DIM:(i + 1) * HEAD_DIM]
    last = SCAN_CHUNK - 1 if u.forward else 0
    c_last = c[last:last + 1]
    u.q_in = (u.q * jnp.exp2(c)).astype(BF16)
    u.k_st = (u.k * jnp.exp2(c_last - c)).astype(BF16)
    u.e_last = jnp.exp2(c_last)
    u.levels = [_pair_level(u.q, u.k, c, b, u.forward) for b in _scan_levels()]
    u.levels.append(_diag_level(u.q, u.k, c, u.forward))


def _stage_scores(u):
    a = None
    for li, (qd, kd) in enumerate(u.levels):
        slot = u.slot * len(u.levels) + li
        u.kt_ref[slot] = kd.T
        a_l = _dot(qd, u.kt_ref[slot])
        a = a_l if li == 0 else jnp.where(u.masks[li], a_l, a)
    u.a16 = a.astype(BF16)


def _stage_output(u):
    contract_lanes = (((1,), (1,)), ((), ()))
    contract_rows = (((0,), (0,)), ((), ()))
    st = u.st_ref[u.h]
    o = _dot(u.a16, u.v16)
    o = o + lax.dot_general(u.q_in, st.astype(BF16), contract_lanes, preferred_element_type=F32)
    u.o_ref[0, u.rows, u.hs] = o.astype(BF16)
    u.st_ref[u.h] = st * u.e_last + lax.dot_general(u.v16, u.k_st, contract_rows,
                                                    preferred_element_type=F32)


def _scan_units(fwd_refs, bwd_refs, kt_ref, tri_f, tri_b):
    n_chunks = SCAN_BLOCK // SCAN_CHUNK
    masks = {True: _level_masks(True), False: _level_masks(False)}
    units = []
    for ci in range(n_chunks):
        for refs, tri, forward in ((fwd_refs, tri_f, True), (bwd_refs, tri_b, False)):
            c = ci if forward else n_chunks - 1 - ci
            rows = slice(c * SCAN_CHUNK, (c + 1) * SCAN_CHUNK)
            for h in range(HEADS):
                units.append(_Unit(*refs, kt_ref, len(units) % KT_SLOTS, rows, h, tri,
                                   masks[forward], forward))
    return units


def _scan_kernel(*refs, layer, n_casts):
    lb_ref, mf_ref, mb_ref, qf_ref, vf_ref, zf_ref, qb_ref, vb_ref, zb_ref = refs[:9]
    cast_in = refs[9:9 + n_casts]
    of_ref, ob_ref = refs[9 + n_casts:11 + n_casts]
    cast_out = refs[11 + n_casts:11 + 2 * n_casts]
    sf_ref, sb_ref, kt_ref = refs[11 + 2 * n_casts:]

    @pl.when(pl.program_id(1) == 0)
    def _():
        sf_ref[...] = jnp.zeros_like(sf_ref)
        sb_ref[...] = jnp.zeros_like(sb_ref)

    for src_ref, dst_ref in zip(cast_in, cast_out):
        dst_ref[...] = src_ref[...].astype(BF16)

    raw = lb_ref[...]
    ex = jnp.exp(raw - jnp.max(raw, axis=0, keepdims=True))
    soft = ex / jnp.sum(ex, axis=0, keepdims=True)
    lb = jnp.zeros((1, D_MODEL), F32)
    for i in range(1, layer + 1):
        lb = lb + soft[i:i + 1]

    units = _scan_units((qf_ref, vf_ref, zf_ref, of_ref, sf_ref), (qb_ref, vb_ref, zb_ref, ob_ref, sb_ref),
                        kt_ref, mf_ref[...], mb_ref[...])
    stages = (functools.partial(_stage_gates, lb=lb), _stage_levels, _stage_scores, _stage_output)
    _emit_pipeline(len(units), [(off, lambda n, fn=fn: fn(units[n])) for off, fn in zip(SCAN_SKEW, stages)])


def _cast_block_rows(rows, n_steps):
    packed_rows = 16
    for block in range(packed_rows, rows + 1, packed_rows):
        if rows % block == 0 and rows // block <= n_steps:
            return block
    raise ValueError(f"cannot spread {rows} rows over {n_steps} steps")


def _scan(q, v, zf, zb, hgrn_lb, layer, casts):
    bsz, s, _ = q.shape
    c = SCAN_BLOCK
    n_steps = s // c
    n_ctx = CTX_LEN // c
    mf = jnp.asarray(_cumsum_matrix(True), BF16)
    mb = jnp.asarray(_cumsum_matrix(False), BF16)

    def fwd_block(i):
        return jnp.where(i < n_ctx, n_steps - n_ctx + i, i - n_ctx)

    def bwd_block(i):
        return n_steps - 1 - i

    def rows(block_of):
        return pl.BlockSpec((1, c, D_MODEL), lambda b, i: (b, block_of(i), 0))

    fwd, bwd = rows(fwd_block), rows(bwd_block)
    out_sds = jax.ShapeDtypeStruct((bsz, s, D_MODEL), BF16)

    cast_in, cast_out, cast_sds = [], [], []
    for w, row0, n_rows in casts:
        block = _cast_block_rows(n_rows, bsz * n_steps)
        first, last = row0 // block, n_rows // block - 1

        def step_block(b, i, last=last):
            return jnp.minimum(b * n_steps + i, last)

        cast_in.append(pl.BlockSpec((block, w.shape[1]),
                                    lambda b, i, f=first, sb=step_block: (f + sb(b, i), 0)))
        cast_out.append(pl.BlockSpec((block, w.shape[1]), lambda b, i, sb=step_block: (sb(b, i), 0)))
        cast_sds.append(jax.ShapeDtypeStruct((n_rows, w.shape[1]), BF16))

    out = pl.pallas_call(
        functools.partial(_scan_kernel, layer=layer, n_casts=len(casts)),
        grid=(bsz, n_steps),
        in_specs=[_resident(hgrn_lb), _resident(mf), _resident(mb), fwd, fwd, fwd, bwd, bwd, bwd] + cast_in,
        out_specs=[fwd, bwd] + cast_out,
        out_shape=[out_sds, out_sds] + cast_sds,
        scratch_shapes=[pltpu.VMEM((HEADS, HEAD_DIM, HEAD_DIM), F32),
                        pltpu.VMEM((HEADS, HEAD_DIM, HEAD_DIM), F32),
                        pltpu.VMEM((KT_SLOTS * (len(_scan_levels()) + 1), HEAD_DIM, SCAN_CHUNK), BF16)],
        compiler_params=_params(2),
        name="hgrn_scan",
    )(hgrn_lb, mf, mb, q, v, zf, q, v, zb, *[w for w, _, _ in casts])
    return out[0], out[1], out[2:]


def _readout_ffn_kernel(x_ref, of_ref, ob_ref, g_ref, mod_ref, nw_ref, gn_ref, w_ref, wi_ref, wo_ref,
                        o_ref, *, n_tiles):
    b, i = pl.program_id(0), pl.program_id(1)
    tail = _FfnTail(x_ref, o_ref, mod_ref, nw_ref, wi_ref, wo_ref, b, _sub_blocks(i, n_tiles))
    mixed = [None] * len(tail.blocks)

    def mix(r):
        rows, _ = tail.blocks[r]
        o = of_ref[0, rows].astype(F32) + ob_ref[0, rows].astype(F32)
        gn = gn_ref[...]
        heads = [_rms(o[:, h * HEAD_DIM:(h + 1) * HEAD_DIM]) * gn for h in range(HEADS)]
        y = jnp.concatenate(heads, axis=-1) * _silu(g_ref[0, rows].astype(F32))
        mixed[r] = y.astype(BF16)

    def project_out(r):
        tail.ys[r] = _dot(mixed[r], w_ref[...])

    tail.emit([mix, project_out])


def _readout_ffn(xs, o_f, o_b, g, mod_all, norm_w, gnorm_all, w_out_all, ffn_w_in, ffn_w_out,
                 layer, j, ffn_idx, out_rows):
    bsz, s, d = xs.shape
    n_tiles = s // ROW_TILE
    tile = pl.BlockSpec((1, ROW_TILE, d), lambda b, i: (b, i, 0))
    return pl.pallas_call(
        functools.partial(_readout_ffn_kernel, n_tiles=n_tiles),
        grid=(bsz, n_tiles),
        in_specs=[tile, tile, tile, tile, _resident(mod_all, layer), _resident(norm_w, layer),
                  _resident(gnorm_all, j), _resident(w_out_all, j),
                  _resident(ffn_w_in, ffn_idx), _resident(ffn_w_out, ffn_idx)],
        out_specs=tile,
        out_shape=jax.ShapeDtypeStruct((bsz, out_rows, d), F32),
        compiler_params=_params(2),
        name="hgrn_readout_ffn",
    )(xs, o_f, o_b, g, mod_all, norm_w, gnorm_all, w_out_all, ffn_w_in, ffn_w_out)


def _conv_ffn_kernel(x_ref, mod_ref, nw_ref, ci_ref, cw_ref, co_ref, wi_ref, wo_ref, o_ref,
                     *, n_tiles, seq):
    b, i = pl.program_id(0), pl.program_id(1)
    d = D_MODEL
    tail = _FfnTail(x_ref, o_ref, mod_ref, nw_ref, wi_ref, wo_ref, b, _sub_blocks(i, n_tiles))
    n_blk = len(tail.blocks)
    hs, us, gs, ts = ([None] * n_blk for _ in range(4))

    def pre(r):
        rows, is_ctx = tail.blocks[r]
        hs[r] = _prenorm(x_ref[0, rows], nw_ref[0:1, :], mod_ref, b, is_ctx, 0).astype(BF16)

    def project(r):
        gs[r] = _dot(hs[r], ci_ref[:, 0:d])
        us[r] = _dot(hs[r], ci_ref[:, d:2 * d]) * _dot(hs[r], ci_ref[:, 2 * d:3 * d])

    def convolve(r):
        rows, _ = tail.blocks[r]
        u = us[r]
        pos = i * ROW_TILE + rows.start + lax.broadcasted_iota(jnp.int32, (SUB_ROWS, 1), 0)
        in_ctx = pos >= seq
        has_prev = (pos != seq) & (in_ctx | (pos % GRID_W != 0))
        has_next = (pos != seq + CTX_LEN - 1) & (in_ctx | (pos % GRID_W != GRID_W - 1))
        u_prev = jnp.where(has_prev, pltpu.roll(u, 1, axis=0), 0.0)
        u_next = jnp.where(has_next, pltpu.roll(u, SUB_ROWS - 1, axis=0), 0.0)
        y = u_prev * cw_ref[0:1, :] + u * cw_ref[1:2, :] + u_next * cw_ref[2:3, :]
        ts[r] = (gs[r] * y).astype(BF16)

    def project_out(r):
        tail.ys[r] = _dot(ts[r], co_ref[...])

    tail.emit([pre, project, convolve, project_out])


def _conv_ffn(xs, mod_all, norm_w, w_in, conv_w_all, w_out_all, ffn_w_in, ffn_w_out,
              layer, j, ffn_idx, out_rows):
    bsz, s, d = xs.shape
    n_tiles = s // ROW_TILE
    tile = pl.BlockSpec((1, ROW_TILE, d), lambda b, i: (b, i, 0))
    return pl.pallas_call(
        functools.partial(_conv_ffn_kernel, n_tiles=n_tiles, seq=s - CTX_LEN),
        grid=(bsz, n_tiles),
        in_specs=[tile, _resident(mod_all, layer), _resident(norm_w, layer), _resident(w_in, 0),
                  _resident(conv_w_all, j), _resident(w_out_all, j),
                  _resident(ffn_w_in, ffn_idx), _resident(ffn_w_out, ffn_idx)],
        out_specs=tile,
        out_shape=jax.ShapeDtypeStruct((bsz, out_rows, d), F32),
        compiler_params=_params(2),
        name="conv_ffn",
    )(xs, mod_all, norm_w, w_in, conv_w_all, w_out_all, ffn_w_in, ffn_w_out)


class _FfnTail:
    def __init__(self, x_ref, o_ref, mod_ref, nw_ref, wi_ref, wo_ref, batch, blocks):
        self.x_ref, self.o_ref, self.mod_ref, self.nw_ref = x_ref, o_ref, mod_ref, nw_ref
        self.wi_ref, self.wo_ref, self.batch, self.blocks = wi_ref, wo_ref, batch, blocks
        self.hidden = wo_ref.shape[0]
        self.tf = self.hidden // FFN_SPLIT
        n_blk = len(blocks)
        self.ys, self.x1, self.hs, self.accs = ([None] * n_blk for _ in range(4))
        self.gates, self.ups, self.acts = ([None] * (n_blk * FFN_SPLIT) for _ in range(3))

    def residual(self, r):
        rows, is_ctx = self.blocks[r]
        x1 = _postnorm_residual(self.x_ref[0, rows], self.ys[r], self.nw_ref[1:2, :],
                                self.mod_ref, self.batch, is_ctx, 2)
        self.x1[r] = x1
        self.hs[r] = _prenorm(x1, self.nw_ref[2:3, :], self.mod_ref, self.batch, is_ctx, 3).astype(BF16)

    def project(self, n):
        r, f = divmod(n, FFN_SPLIT)
        lo, hi = f * self.tf, (f + 1) * self.tf
        self.gates[n] = _dot(self.hs[r], self.wi_ref[:, lo:hi])
        self.ups[n] = _dot(self.hs[r], self.wi_ref[:, self.hidden + lo:self.hidden + hi])

    def activate(self, n):
        self.acts[n] = (_silu(self.gates[n]) * self.ups[n]).astype(BF16)
        self.gates[n] = self.ups[n] = None

    def down(self, n):
        r, f = divmod(n, FFN_SPLIT)
        y = _dot(self.acts[n], self.wo_ref[f * self.tf:(f + 1) * self.tf, :])
        self.accs[r] = y if f == 0 else self.accs[r] + y
        self.acts[n] = None

    def finish(self, r):
        rows, is_ctx = self.blocks[r]
        self.o_ref[0, rows] = _postnorm_residual(self.x1[r], self.accs[r], self.nw_ref[3:4, :],
                                                 self.mod_ref, self.batch, is_ctx, 5)

    def emit(self, mixer_stages):
        def on_first(fn):
            return lambda n: fn(n // FFN_SPLIT) if n % FFN_SPLIT == 0 else None

        def on_last(fn):
            return lambda n: fn(n // FFN_SPLIT) if n % FFN_SPLIT == FFN_SPLIT - 1 else None

        stages = [on_first(fn) for fn in mixer_stages]
        stages += [on_first(self.residual), self.project, self.activate, self.down, on_last(self.finish)]
        _emit_pipeline(len(self.blocks) * FFN_SPLIT, list(enumerate(stages)))


def kernel(x, c, ctx, c_ctx, ada_w, ada_b, norm_w, hgrn_w_in, hgrn_w_out, hgrn_gnorm, hgrn_lb,
           conv_w_in, conv_w, conv_w_out, ffn_w_in, ffn_w_out):
    bsz, seq, d = x.shape
    depth = ada_w.shape[0]
    s_all = seq + CTX_LEN
    assert d == D_MODEL and ctx.shape[1] == CTX_LEN and bsz < MOD_ROWS and depth % 2 == 0
    assert s_all % ROW_TILE == 0 and ROW_TILE % SUB_ROWS == 0 and SUB_ROWS % GRID_W == 0
    assert seq % SCAN_BLOCK == 0 and CTX_LEN % SCAN_BLOCK == 0 and seq % GRID_W == 0

    c_rows = jnp.zeros((MOD_ROWS, d), F32).at[:bsz].set(c).at[MOD_ROWS - 1].set(c_ctx)
    mod_all = _ada_table(c_rows, ada_w, ada_b)

    d_ffn = ffn_w_in.shape[2]
    hidden = ffn_w_out.shape[1]
    n_rec = hgrn_w_in.shape[0]
    w_in_first = hgrn_w_in[0].astype(BF16)
    hgrn_w_out, conv_w_out = hgrn_w_out.astype(BF16), conv_w_out.astype(BF16)
    hgrn_w_in_rows = hgrn_w_in.reshape(n_rec * d, -1)
    conv_w_in_rows = conv_w_in.reshape(-1, conv_w_in.shape[2])
    ffn_w_in_rows = ffn_w_in.reshape(depth * d, d_ffn)
    ffn_w_out_rows = ffn_w_out.reshape(depth * hidden, d)
    gnorm = hgrn_gnorm.reshape(-1, 1, HEAD_DIM)

    xs, w_in_next = None, w_in_first
    for j in range(n_rec):
        l = 2 * j
        first = xs is None
        (q, v, zf, zb, g), xs = _in_proj(x if first else xs, ctx if first else None,
                                         mod_all, norm_w, w_in_next, l)
        casts = [(ffn_w_in_rows, l * d, 2 * d), (ffn_w_out_rows, l * hidden, 2 * hidden),
                 (conv_w_in_rows, j * d, d)]
        if j + 1 < n_rec:
            casts.append((hgrn_w_in_rows, (j + 1) * d, d))
        o_f, o_b, cast = _scan(q, v, zf, zb, hgrn_lb, j, casts)
        ffn_in = cast[0].reshape(2, d, d_ffn)
        ffn_out = cast[1].reshape(2, hidden, d)
        conv_in = cast[2].reshape(1, d, -1)
        w_in_next = cast[3] if j + 1 < n_rec else None
        xs = _readout_ffn(xs, o_f, o_b, g, mod_all, norm_w, gnorm, hgrn_w_out, ffn_in, ffn_out,
                          l, j, 0, s_all)
        xs = _conv_ffn(xs, mod_all, norm_w, conv_in, conv_w, conv_w_out, ffn_in, ffn_out,
                       l + 1, j, 1, seq if l + 1 == depth - 1 else s_all)
    return xs
```

```python
import functools

import numpy as np
import jax
import jax.numpy as jnp
from jax import lax
from jax.experimental import pallas as pl
from jax.experimental.pallas import tpu as pltpu

D_MODEL = 1024
CTX_LEN = 256
GRID_W = 64
HEADS = 8
HEAD_DIM = D_MODEL // HEADS
CONV_WIDTH = 3
N_MOD = 6
EPS = 1e-6
F_FLOOR = 1e-6

SCAN_BLOCK = 256
SCAN_CHUNK = 128
SCAN_BASE = 8
SCAN_SKEW = (0, 1, 2, 3)
CAST_EVERY = 4
CUMSUM_PARTS = 2
ROW_TILE = 768
SUB_ROWS = CTX_LEN
FFN_SPLIT = 2
MOD_ROWS = 8
VMEM_LIMIT = 56 * 1024 * 1024

F32 = jnp.float32
BF16 = jnp.bfloat16


def _bf16_parts(x, n):
    parts = []
    r = x
    for i in range(n):
        p = r.astype(BF16)
        parts.append(p)
        if i + 1 < n:
            r = r - p.astype(F32)
    return parts


def _dot(a, b):
    return jnp.dot(a, b, preferred_element_type=F32)


def _rms(x):
    return x * lax.rsqrt(jnp.mean(x * x, axis=-1, keepdims=True) + EPS)


def _silu(x):
    return x * (1.0 / (1.0 + jnp.exp(-x)))


def _emit_pipeline(n_items, stages):
    depth = max(off for off, _ in stages)
    for t in range(n_items + depth):
        for off, fn in stages:
            if 0 <= t - off < n_items:
                fn(t - off)


def _sub_blocks(tile_idx, n_tiles):
    n = ROW_TILE // SUB_ROWS
    blocks = []
    for r in range(n):
        rows = slice(r * SUB_ROWS, (r + 1) * SUB_ROWS)
        blocks.append((rows, (tile_idx == n_tiles - 1) if r == n - 1 else False))
    return blocks


def _mod_vec(mod_ref, batch, is_ctx, idx):
    row = batch if is_ctx is False else jnp.where(is_ctx, MOD_ROWS - 1, batch)
    return mod_ref[pl.ds(row, 1), idx * D_MODEL:(idx + 1) * D_MODEL]


def _prenorm(x, nw, mod_ref, batch, is_ctx, shift_idx):
    shift = _mod_vec(mod_ref, batch, is_ctx, shift_idx)
    scale = _mod_vec(mod_ref, batch, is_ctx, shift_idx + 1)
    return _rms(x) * (nw * (1.0 + scale)) + shift


def _postnorm_residual(x, y, nw, mod_ref, batch, is_ctx, gate_idx):
    gate = _mod_vec(mod_ref, batch, is_ctx, gate_idx)
    return x + _rms(y) * (gate * nw)


def _params(n_axes):
    return pltpu.CompilerParams(dimension_semantics=("arbitrary",) * n_axes,
                                vmem_limit_bytes=VMEM_LIMIT)


def _resident(a, layer=None):
    if layer is None:
        shape, index = a.shape, (0,) * a.ndim
    else:
        shape, index = (None,) + a.shape[1:], (layer,) + (0,) * (a.ndim - 1)
    return pl.BlockSpec(shape, lambda *_: index, pipeline_mode=pl.Buffered(1))


def _ada_kernel(c_ref, w_ref, b_ref, o_ref):
    a_hi, a_lo = _bf16_parts(_silu(c_ref[...]), 2)
    w_hi, w_lo = _bf16_parts(w_ref[0], 2)
    o_ref[0] = _dot(a_hi, w_hi) + _dot(a_lo, w_hi) + _dot(a_hi, w_lo) + b_ref[0]


def _ada_table(c_rows, ada_w, ada_b):
    depth, d, n = ada_w.shape
    tn = 1536
    return pl.pallas_call(
        _ada_kernel,
        grid=(depth, n // tn),
        in_specs=[
            pl.BlockSpec((MOD_ROWS, d), lambda l, j: (0, 0)),
            pl.BlockSpec((1, d, tn), lambda l, j: (l, 0, j)),
            pl.BlockSpec((1, 1, tn), lambda l, j: (l, 0, j)),
        ],
        out_specs=pl.BlockSpec((1, MOD_ROWS, tn), lambda l, j: (l, 0, j)),
        out_shape=jax.ShapeDtypeStruct((depth, MOD_ROWS, n), F32),
        compiler_params=_params(2),
        name="ada_table",
    )(c_rows, ada_w, ada_b.reshape(depth, 1, n))


def _in_proj_kernel(*refs, joins, n_tiles):
    n_blk = ROW_TILE // SUB_ROWS
    x_refs, refs = refs[:n_blk], refs[n_blk:]
    ctx_ref = refs[0] if joins else None
    mod_ref, nw_ref, w_ref, q_ref, v_ref, zf_ref, zb_ref, g_ref = refs[joins:joins + 8]
    xs_ref = refs[-1] if joins else None
    b, i = pl.program_id(0), pl.program_id(1)
    d = D_MODEL
    blocks = _sub_blocks(i, n_tiles)
    hs = [None] * n_blk

    def pre(n):
        rows, is_ctx = blocks[n]
        x = x_refs[n][0]
        if joins:
            if is_ctx is not False:
                x = jnp.where(is_ctx, ctx_ref[0], x)
            xs_ref[0, rows] = x
        hs[n] = _prenorm(x, nw_ref[0:1, :], mod_ref, b, is_ctx, 0).astype(BF16)

    def project(n):
        rows, _ = blocks[n]
        h = hs[n]
        q_ref[0, rows] = (_dot(h, w_ref[:, 0:d]) * (HEAD_DIM ** -0.5)).astype(BF16)
        v_ref[0, rows] = _dot(h, w_ref[:, d:2 * d]).astype(BF16)
        zf_ref[0, rows] = _dot(h, w_ref[:, 2 * d:3 * d])
        zb_ref[0, rows] = _dot(h, w_ref[:, 3 * d:4 * d])
        g_ref[0, rows] = _dot(h, w_ref[:, 4 * d:5 * d]).astype(BF16)

    _emit_pipeline(n_blk, [(0, pre), (1, project)])


def _in_proj(x, ctx, mod_all, norm_w, w, layer):
    bsz, _, d = x.shape
    joins = ctx is not None
    s = x.shape[1] + (CTX_LEN if joins else 0)
    n_tiles = s // ROW_TILE
    n_blk = ROW_TILE // SUB_ROWS
    last_block = x.shape[1] // SUB_ROWS - 1
    tile = pl.BlockSpec((1, ROW_TILE, d), lambda b, i: (b, i, 0))

    def x_block(n):
        return pl.BlockSpec((1, SUB_ROWS, d), lambda b, i: (b, jnp.minimum(i * n_blk + n, last_block), 0))

    in_specs = [x_block(n) for n in range(n_blk)]
    in_specs += [pl.BlockSpec((1, CTX_LEN, d), lambda b, i: (b, 0, 0))] if joins else []
    in_specs += [_resident(mod_all, layer), _resident(norm_w, layer), _resident(w)]
    sds16 = jax.ShapeDtypeStruct((bsz, s, d), BF16)
    sds32 = jax.ShapeDtypeStruct((bsz, s, d), F32)
    out = pl.pallas_call(
        functools.partial(_in_proj_kernel, joins=joins, n_tiles=n_tiles),
        grid=(bsz, n_tiles),
        in_specs=in_specs,
        out_specs=[tile] * (6 if joins else 5),
        out_shape=[sds16, sds16, sds32, sds32, sds16] + ([sds32] if joins else []),
        compiler_params=_params(2),
        name="hgrn_in_proj",
    )(*([x] * n_blk), *([ctx] if joins else []), mod_all, norm_w, w)
    return (out[:5], out[5]) if joins else (out, x)


def _scan_levels():
    levels = []
    b = SCAN_CHUNK // 2
    while b >= SCAN_BASE:
        levels.append(b)
        b //= 2
    return levels


def _cumsum_matrix(forward):
    t = np.arange(SCAN_CHUNK)
    tri = t[None, :] <= t[:, None] if forward else t[None, :] >= t[:, None]
    return tri.astype(np.float32)


def _level_masks(forward):
    c = SCAN_CHUNK
    t = lax.broadcasted_iota(jnp.int32, (c, c), 0)
    s = lax.broadcasted_iota(jnp.int32, (c, c), 1)
    if not forward:
        t, s = (c - 1) - t, (c - 1) - s
    masks = []
    for b in _scan_levels():
        same = (t // (2 * b)) == (s // (2 * b))
        masks.append(same & (t % (2 * b) >= b) & (s % (2 * b) < b))
    same = (t // SCAN_BASE) == (s // SCAN_BASE)
    masks.append(same & (s <= t))
    return masks


def _pair_level(q, k, c, half, forward):
    packed_rows = 16
    qs, ks = [], []
    for start in range(0, SCAN_CHUNK, 2 * half):
        lo, hi = slice(start, start + half), slice(start + half, start + 2 * half)
        if forward:
            ref, q_rows, k_rows = c[start + half - 1:start + half], hi, lo
        else:
            ref, q_rows, k_rows = c[start + half:start + half + 1], lo, hi
        eq = jnp.exp2(c[q_rows] - ref)
        ek = jnp.exp2(ref - c[k_rows])
        if half % packed_rows == 0:
            zero = jnp.zeros((half, c.shape[-1]), BF16)
            qe, ke = (q[q_rows] * eq).astype(BF16), (k[k_rows] * ek).astype(BF16)
            qs += [zero, qe] if forward else [qe, zero]
            ks += [ke, zero] if forward else [zero, ke]
        else:
            e = jnp.concatenate([ek, eq] if forward else [eq, ek], axis=0)
            rows = slice(start, start + 2 * half)
            qs.append((q[rows] * e).astype(BF16))
            ks.append((k[rows] * e).astype(BF16))
    return jnp.concatenate(qs, axis=0), jnp.concatenate(ks, axis=0)


def _diag_level(q, k, c, forward):
    c3 = c.reshape(SCAN_CHUNK // SCAN_BASE, SCAN_BASE, c.shape[-1])
    r = SCAN_BASE // 2 - 1 if forward else SCAN_BASE // 2
    x = (c3 - c3[:, r:r + 1, :]).reshape(c.shape)
    return (q * jnp.exp2(x)).astype(BF16), (k * jnp.exp2(-x)).astype(BF16)


class _Unit:
    def __init__(self, q_ref, v_ref, z_ref, o_ref, st_ref, rows, h, tri, masks, forward):
        self.q_ref, self.v_ref, self.z_ref, self.o_ref, self.st_ref = q_ref, v_ref, z_ref, o_ref, st_ref
        self.rows, self.h, self.tri, self.masks, self.forward = rows, h, tri, masks, forward
        self.hs = slice(h * HEAD_DIM, (h + 1) * HEAD_DIM)


def _stage_gates(u, lb):
    lb = lb[:, u.hs]
    rest = 1.0 - lb
    z = u.z_ref[0, u.rows, u.hs]
    sig = 1.0 / (1.0 + jnp.exp(-z))
    gate = rest * sig
    g = jnp.log(jnp.maximum(lb + gate, F_FLOOR)) * (1.0 / np.log(2.0))
    u.k = rest - gate
    u.q = u.q_ref[0, u.rows, u.hs].astype(F32)
    u.v16 = u.v_ref[0, u.rows, u.hs]
    u.cs = _dot(u.tri, jnp.concatenate(_bf16_parts(g, CUMSUM_PARTS), axis=-1))


def _stage_levels(u):
    c = u.cs[:, :HEAD_DIM]
    for i in range(1, CUMSUM_PARTS):
        c = c + u.cs[:, i * HEAD_DIM:(i + 1) * HEAD_DIM]
    last = SCAN_CHUNK - 1 if u.forward else 0
    c_last = c[last:last + 1]
    u.q_in = (u.q * jnp.exp2(c)).astype(BF16)
    u.k_st = (u.k * jnp.exp2(c_last - c)).astype(BF16)
    u.e_last = jnp.exp2(c_last)
    u.levels = [_pair_level(u.q, u.k, c, b, u.forward) for b in _scan_levels()]
    u.levels.append(_diag_level(u.q, u.k, c, u.forward))


def _stage_scores(u):
    a = None
    for li, (qd, kd) in enumerate(u.levels):
        a_l = lax.dot_general(qd, kd, (((1,), (1,)), ((), ())), preferred_element_type=F32)
        a = a_l if li == 0 else jnp.where(u.masks[li], a_l, a)
    u.a16 = a.astype(BF16)


def _stage_output(u):
    contract_lanes = (((1,), (1,)), ((), ()))
    contract_rows = (((0,), (0,)), ((), ()))
    st = u.st_ref[u.h]
    o = _dot(u.a16, u.v16)
    o = o + lax.dot_general(u.q_in, st.astype(BF16), contract_lanes, preferred_element_type=F32)
    u.o_ref[0, u.rows, u.hs] = o.astype(BF16)
    u.st_ref[u.h] = st * u.e_last + lax.dot_general(u.v16, u.k_st, contract_rows,
                                                    preferred_element_type=F32)


def _scan_units(fwd_refs, bwd_refs, tri_f, tri_b):
    n_chunks = SCAN_BLOCK // SCAN_CHUNK
    masks = {True: _level_masks(True), False: _level_masks(False)}
    units = []
    for ci in range(n_chunks):
        for refs, tri, forward in ((fwd_refs, tri_f, True), (bwd_refs, tri_b, False)):
            c = ci if forward else n_chunks - 1 - ci
            rows = slice(c * SCAN_CHUNK, (c + 1) * SCAN_CHUNK)
            for h in range(HEADS):
                units.append(_Unit(*refs, rows, h, tri, masks[forward], forward))
    return units


def _scan_kernel(*refs, layer, n_casts):
    lb_ref, mf_ref, mb_ref, qf_ref, vf_ref, zf_ref, qb_ref, vb_ref, zb_ref = refs[:9]
    cast_in = refs[9:9 + n_casts]
    of_ref, ob_ref = refs[9 + n_casts:11 + n_casts]
    cast_out = refs[11 + n_casts:11 + 2 * n_casts]
    sf_ref, sb_ref = refs[11 + 2 * n_casts:]

    @pl.when(pl.program_id(1) == 0)
    def _():
        sf_ref[...] = jnp.zeros_like(sf_ref)
        sb_ref[...] = jnp.zeros_like(sb_ref)

    step = pl.program_id(0) * pl.num_programs(1) + pl.program_id(1)

    @pl.when(step % CAST_EVERY == 0)
    def _():
        for src_ref, dst_ref in zip(cast_in, cast_out):
            dst_ref[...] = src_ref[...].astype(BF16)

    raw = lb_ref[...]
    ex = jnp.exp(raw - jnp.max(raw, axis=0, keepdims=True))
    soft = ex / jnp.sum(ex, axis=0, keepdims=True)
    lb = jnp.zeros((1, D_MODEL), F32)
    for i in range(1, layer + 1):
        lb = lb + soft[i:i + 1]

    units = _scan_units((qf_ref, vf_ref, zf_ref, of_ref, sf_ref), (qb_ref, vb_ref, zb_ref, ob_ref, sb_ref),
                        mf_ref[...], mb_ref[...])
    stages = (functools.partial(_stage_gates, lb=lb), _stage_levels, _stage_scores, _stage_output)
    _emit_pipeline(len(units), [(off, lambda n, fn=fn: fn(units[n])) for off, fn in zip(SCAN_SKEW, stages)])


def _cast_block_rows(rows, n_blocks):
    packed_rows = 16
    for block in range(packed_rows, rows + 1, packed_rows):
        if rows % block == 0 and rows // block <= n_blocks:
            return block
    raise ValueError(f"cannot spread {rows} rows over {n_blocks} blocks")


def _scan(q, v, zf, zb, hgrn_lb, layer, casts):
    bsz, s, _ = q.shape
    c = SCAN_BLOCK
    n_steps = s // c
    n_ctx = CTX_LEN // c
    mf = jnp.asarray(_cumsum_matrix(True), BF16)
    mb = jnp.asarray(_cumsum_matrix(False), BF16)

    def fwd_block(i):
        return jnp.where(i < n_ctx, n_steps - n_ctx + i, i - n_ctx)

    def bwd_block(i):
        return n_steps - 1 - i

    def rows(block_of):
        return pl.BlockSpec((1, c, D_MODEL), lambda b, i: (b, block_of(i), 0))

    fwd, bwd = rows(fwd_block), rows(bwd_block)
    out_sds = jax.ShapeDtypeStruct((bsz, s, D_MODEL), BF16)

    cast_in, cast_out, cast_sds = [], [], []
    for w, row0, n_rows in casts:
        block = _cast_block_rows(n_rows, bsz * n_steps // CAST_EVERY)
        first, last = row0 // block, n_rows // block - 1

        def step_block(b, i, last=last):
            return jnp.minimum((b * n_steps + i) // CAST_EVERY, last)

        cast_in.append(pl.BlockSpec((block, w.shape[1]),
                                    lambda b, i, f=first, sb=step_block: (f + sb(b, i), 0)))
        cast_out.append(pl.BlockSpec((block, w.shape[1]), lambda b, i, sb=step_block: (sb(b, i), 0)))
        cast_sds.append(jax.ShapeDtypeStruct((n_rows, w.shape[1]), BF16))

    out = pl.pallas_call(
        functools.partial(_scan_kernel, layer=layer, n_casts=len(casts)),
        grid=(bsz, n_steps),
        in_specs=[_resident(hgrn_lb), _resident(mf), _resident(mb), fwd, fwd, fwd, bwd, bwd, bwd] + cast_in,
        out_specs=[fwd, bwd] + cast_out,
        out_shape=[out_sds, out_sds] + cast_sds,
        scratch_shapes=[pltpu.VMEM((HEADS, HEAD_DIM, HEAD_DIM), F32),
                        pltpu.VMEM((HEADS, HEAD_DIM, HEAD_DIM), F32)],
        compiler_params=_params(2),
        name="hgrn_scan",
    )(hgrn_lb, mf, mb, q, v, zf, q, v, zb, *[w for w, _, _ in casts])
    return out[0], out[1], out[2:]


def _readout_ffn_kernel(x_ref, of_ref, ob_ref, g_ref, mod_ref, nw_ref, gn_ref, w_ref, wi_ref, wo_ref,
                        o_ref, *, n_tiles):
    b, i = pl.program_id(0), pl.program_id(1)
    tail = _FfnTail(x_ref, o_ref, mod_ref, nw_ref, wi_ref, wo_ref, b, _sub_blocks(i, n_tiles))
    mixed = [None] * len(tail.blocks)

    def mix(r):
        rows, _ = tail.blocks[r]
        o = of_ref[0, rows].astype(F32) + ob_ref[0, rows].astype(F32)
        gn = gn_ref[...]
        heads = [_rms(o[:, h * HEAD_DIM:(h + 1) * HEAD_DIM]) * gn for h in range(HEADS)]
        y = jnp.concatenate(heads, axis=-1) * _silu(g_ref[0, rows].astype(F32))
        mixed[r] = y.astype(BF16)

    def project_out(r):
        tail.ys[r] = _dot(mixed[r], w_ref[...])

    tail.emit([mix, project_out])


def _readout_ffn(xs, o_f, o_b, g, mod_all, norm_w, gnorm_all, w_out_all, ffn_w_in, ffn_w_out,
                 layer, j, ffn_idx, out_rows):
    bsz, s, d = xs.shape
    n_tiles = s // ROW_TILE
    tile = pl.BlockSpec((1, ROW_TILE, d), lambda b, i: (b, i, 0))
    return pl.pallas_call(
        functools.partial(_readout_ffn_kernel, n_tiles=n_tiles),
        grid=(bsz, n_tiles),
        in_specs=[tile, tile, tile, tile, _resident(mod_all, layer), _resident(norm_w, layer),
                  _resident(gnorm_all, j), _resident(w_out_all, j),
                  _resident(ffn_w_in, ffn_idx), _resident(ffn_w_out, ffn_idx)],
        out_specs=tile,
        out_shape=jax.ShapeDtypeStruct((bsz, out_rows, d), F32),
        compiler_params=_params(2),
        name="hgrn_readout_ffn",
    )(xs, o_f, o_b, g, mod_all, norm_w, gnorm_all, w_out_all, ffn_w_in, ffn_w_out)


def _conv_ffn_kernel(x_ref, mod_ref, nw_ref, ci_ref, cw_ref, co_ref, wi_ref, wo_ref, o_ref,
                     *, n_tiles, seq):
    b, i = pl.program_id(0), pl.program_id(1)
    d = D_MODEL
    tail = _FfnTail(x_ref, o_ref, mod_ref, nw_ref, wi_ref, wo_ref, b, _sub_blocks(i, n_tiles))
    n_blk = len(tail.blocks)
    hs, us, gs, ts = ([None] * n_blk for _ in range(4))

    def pre(r):
        rows, is_ctx = tail.blocks[r]
        hs[r] = _prenorm(x_ref[0, rows], nw_ref[0:1, :], mod_ref, b, is_ctx, 0).astype(BF16)

    def project(r):
        gs[r] = _dot(hs[r], ci_ref[:, 0:d])
        us[r] = _dot(hs[r], ci_ref[:, d:2 * d]) * _dot(hs[r], ci_ref[:, 2 * d:3 * d])

    def convolve(r):
        rows, _ = tail.blocks[r]
        u = us[r]
        pos = i * ROW_TILE + rows.start + lax.broadcasted_iota(jnp.int32, (SUB_ROWS, 1), 0)
        in_ctx = pos >= seq
        has_prev = (pos != seq) & (in_ctx | (pos % GRID_W != 0))
        has_next = (pos != seq + CTX_LEN - 1) & (in_ctx | (pos % GRID_W != GRID_W - 1))
        u_prev = jnp.where(has_prev, pltpu.roll(u, 1, axis=0), 0.0)
        u_next = jnp.where(has_next, pltpu.roll(u, SUB_ROWS - 1, axis=0), 0.0)
        y = u_prev * cw_ref[0:1, :] + u * cw_ref[1:2, :] + u_next * cw_ref[2:3, :]
        ts[r] = (gs[r] * y).astype(BF16)

    def project_out(r):
        tail.ys[r] = _dot(ts[r], co_ref[...])

    tail.emit([pre, project, convolve, project_out])


def _conv_ffn(xs, mod_all, norm_w, w_in, conv_w_all, w_out_all, ffn_w_in, ffn_w_out,
              layer, j, ffn_idx, out_rows):
    bsz, s, d = xs.shape
    n_tiles = s // ROW_TILE
    tile = pl.BlockSpec((1, ROW_TILE, d), lambda b, i: (b, i, 0))
    return pl.pallas_call(
        functools.partial(_conv_ffn_kernel, n_tiles=n_tiles, seq=s - CTX_LEN),
        grid=(bsz, n_tiles),
        in_specs=[tile, _resident(mod_all, layer), _resident(norm_w, layer), _resident(w_in, 0),
                  _resident(conv_w_all, j), _resident(w_out_all, j),
                  _resident(ffn_w_in, ffn_idx), _resident(ffn_w_out, ffn_idx)],
        out_specs=tile,
        out_shape=jax.ShapeDtypeStruct((bsz, out_rows, d), F32),
        compiler_params=_params(2),
        name="conv_ffn",
    )(xs, mod_all, norm_w, w_in, conv_w_all, w_out_all, ffn_w_in, ffn_w_out)


class _FfnTail:
    def __init__(self, x_ref, o_ref, mod_ref, nw_ref, wi_ref, wo_ref, batch, blocks):
        self.x_ref, self.o_ref, self.mod_ref, self.nw_ref = x_ref, o_ref, mod_ref, nw_ref
        self.wi_ref, self.wo_ref, self.batch, self.blocks = wi_ref, wo_ref, batch, blocks
        self.hidden = wo_ref.shape[0]
        self.tf = self.hidden // FFN_SPLIT
        n_blk = len(blocks)
        self.ys, self.x1, self.hs, self.accs = ([None] * n_blk for _ in range(4))
        self.gates, self.ups, self.acts = ([None] * (n_blk * FFN_SPLIT) for _ in range(3))

    def residual(self, r):
        rows, is_ctx = self.blocks[r]
        x1 = _postnorm_residual(self.x_ref[0, rows], self.ys[r], self.nw_ref[1:2, :],
                                self.mod_ref, self.batch, is_ctx, 2)
        self.x1[r] = x1
        self.hs[r] = _prenorm(x1, self.nw_ref[2:3, :], self.mod_ref, self.batch, is_ctx, 3).astype(BF16)

    def project(self, n):
        r, f = divmod(n, FFN_SPLIT)
        lo, hi = f * self.tf, (f + 1) * self.tf
        self.gates[n] = _dot(self.hs[r], self.wi_ref[:, lo:hi])
        self.ups[n] = _dot(self.hs[r], self.wi_ref[:, self.hidden + lo:self.hidden + hi])

    def activate(self, n):
        self.acts[n] = (_silu(self.gates[n]) * self.ups[n]).astype(BF16)
        self.gates[n] = self.ups[n] = None

    def down(self, n):
        r, f = divmod(n, FFN_SPLIT)
        y = _dot(self.acts[n], self.wo_ref[f * self.tf:(f + 1) * self.tf, :])
        self.accs[r] = y if f == 0 else self.accs[r] + y
        self.acts[n] = None

    def finish(self, r):
        rows, is_ctx = self.blocks[r]
        self.o_ref[0, rows] = _postnorm_residual(self.x1[r], self.accs[r], self.nw_ref[3:4, :],
                                                 self.mod_ref, self.batch, is_ctx, 5)

    def emit(self, mixer_stages):
        def on_first(fn):
            return lambda n: fn(n // FFN_SPLIT) if n % FFN_SPLIT == 0 else None

        def on_last(fn):
            return lambda n: fn(n // FFN_SPLIT) if n % FFN_SPLIT == FFN_SPLIT - 1 else None

        stages = [on_first(fn) for fn in mixer_stages]
        stages += [on_first(self.residual), self.project, self.activate, self.down, on_last(self.finish)]
        _emit_pipeline(len(self.blocks) * FFN_SPLIT, list(enumerate(stages)))


def kernel(x, c, ctx, c_ctx, ada_w, ada_b, norm_w, hgrn_w_in, hgrn_w_out, hgrn_gnorm, hgrn_lb,
           conv_w_in, conv_w, conv_w_out, ffn_w_in, ffn_w_out):
    bsz, seq, d = x.shape
    depth = ada_w.shape[0]
    s_all = seq + CTX_LEN
    assert d == D_MODEL and ctx.shape[1] == CTX_LEN and bsz < MOD_ROWS and depth % 2 == 0
    assert s_all % ROW_TILE == 0 and ROW_TILE % SUB_ROWS == 0 and SUB_ROWS % GRID_W == 0
    assert seq % SCAN_BLOCK == 0 and CTX_LEN % SCAN_BLOCK == 0 and seq % GRID_W == 0

    c_rows = jnp.zeros((MOD_ROWS, d), F32).at[:bsz].set(c).at[MOD_ROWS - 1].set(c_ctx)
    mod_all = _ada_table(c_rows, ada_w, ada_b)

    d_ffn = ffn_w_in.shape[2]
    hidden = ffn_w_out.shape[1]
    n_rec = hgrn_w_in.shape[0]
    w_in_first = hgrn_w_in[0].astype(BF16)
    hgrn_w_out, conv_w_out = hgrn_w_out.astype(BF16), conv_w_out.astype(BF16)
    hgrn_w_in_rows = hgrn_w_in.reshape(n_rec * d, -1)
    conv_w_in_rows = conv_w_in.reshape(-1, conv_w_in.shape[2])
    ffn_w_in_rows = ffn_w_in.reshape(depth * d, d_ffn)
    ffn_w_out_rows = ffn_w_out.reshape(depth * hidden, d)
    gnorm = hgrn_gnorm.reshape(-1, 1, HEAD_DIM)

    xs, w_in_next = None, w_in_first
    for j in range(n_rec):
        l = 2 * j
        first = xs is None
        (q, v, zf, zb, g), xs = _in_proj(x if first else xs, ctx if first else None,
                                         mod_all, norm_w, w_in_next, l)
        casts = [(ffn_w_in_rows, l * d, 2 * d), (ffn_w_out_rows, l * hidden, 2 * hidden),
                 (conv_w_in_rows, j * d, d)]
        if j + 1 < n_rec:
            casts.append((hgrn_w_in_rows, (j + 1) * d, d))
        o_f, o_b, cast = _scan(q, v, zf, zb, hgrn_lb, j, casts)
        ffn_in = cast[0].reshape(2, d, d_ffn)
        ffn_out = cast[1].reshape(2, hidden, d)
        conv_in = cast[2].reshape(1, d, -1)
        w_in_next = cast[3] if j + 1 < n_rec else None
        xs = _readout_ffn(xs, o_f, o_b, g, mod_all, norm_w, gnorm, hgrn_w_out, ffn_in, ffn_out,
                          l, j, 0, s_all)
        xs = _conv_ffn(xs, mod_all, norm_w, conv_in, conv_w, conv_w_out, ffn_in, ffn_out,
                       l + 1, j, 1, seq if l + 1 == depth - 1 else s_all)
    return xs
```

```python
import functools

import numpy as np
import jax
import jax.numpy as jnp
from jax import lax
from jax.experimental import pallas as pl
from jax.experimental.pallas import tpu as pltpu

D_MODEL = 1024
CTX_LEN = 256
GRID_W = 64
HEADS = 8
HEAD_DIM = D_MODEL // HEADS
CONV_WIDTH = 3
N_MOD = 6
EPS = 1e-6
F_FLOOR = 1e-6

SCAN_BLOCK = 256
SCAN_CHUNK = 128
SCAN_BASE = 8
SCAN_SKEW = (0, 1, 2, 3)
CAST_EVERY = 4
CUMSUM_PARTS = 2
ROW_TILE = 768
SUB_ROWS = CTX_LEN
FFN_SPLIT = 11
MOD_ROWS = 8
VMEM_LIMIT = 56 * 1024 * 1024

F32 = jnp.float32
BF16 = jnp.bfloat16


def _bf16_parts(x, n):
    parts = []
    r = x
    for i in range(n):
        p = r.astype(BF16)
        parts.append(p)
        if i + 1 < n:
            r = r - p.astype(F32)
    return parts


def _dot(a, b):
    return jnp.dot(a, b, preferred_element_type=F32)


def _rms(x):
    return x * lax.rsqrt(jnp.mean(x * x, axis=-1, keepdims=True) + EPS)


def _silu(x):
    return x * (1.0 / (1.0 + jnp.exp(-x)))


def _emit_pipeline(n_items, stages):
    depth = max(off for off, _ in stages)
    for t in range(n_items + depth):
        for off, fn in stages:
            if 0 <= t - off < n_items:
                fn(t - off)


def _sub_blocks(tile_idx, n_tiles):
    n = ROW_TILE // SUB_ROWS
    blocks = []
    for r in range(n):
        rows = slice(r * SUB_ROWS, (r + 1) * SUB_ROWS)
        blocks.append((rows, (tile_idx == n_tiles - 1) if r == n - 1 else False))
    return blocks


def _mod_vec(mod_ref, batch, is_ctx, idx):
    row = batch if is_ctx is False else jnp.where(is_ctx, MOD_ROWS - 1, batch)
    return mod_ref[pl.ds(row, 1), idx * D_MODEL:(idx + 1) * D_MODEL]


def _prenorm(x, nw, mod_ref, batch, is_ctx, shift_idx):
    shift = _mod_vec(mod_ref, batch, is_ctx, shift_idx)
    scale = _mod_vec(mod_ref, batch, is_ctx, shift_idx + 1)
    return _rms(x) * (nw * (1.0 + scale)) + shift


def _postnorm_residual(x, y, nw, mod_ref, batch, is_ctx, gate_idx):
    gate = _mod_vec(mod_ref, batch, is_ctx, gate_idx)
    return x + _rms(y) * (gate * nw)


def _params(n_axes):
    return pltpu.CompilerParams(dimension_semantics=("arbitrary",) * n_axes,
                                vmem_limit_bytes=VMEM_LIMIT)


def _resident(a, layer=None):
    if layer is None:
        shape, index = a.shape, (0,) * a.ndim
    else:
        shape, index = (None,) + a.shape[1:], (layer,) + (0,) * (a.ndim - 1)
    return pl.BlockSpec(shape, lambda *_: index, pipeline_mode=pl.Buffered(1))


def _ada_kernel(c_ref, w_ref, b_ref, o_ref):
    a_hi, a_lo = _bf16_parts(_silu(c_ref[...]), 2)
    w_hi, w_lo = _bf16_parts(w_ref[0], 2)
    o_ref[0] = _dot(a_hi, w_hi) + _dot(a_lo, w_hi) + _dot(a_hi, w_lo) + b_ref[0]


def _ada_table(c_rows, ada_w, ada_b):
    depth, d, n = ada_w.shape
    tn = 1536
    return pl.pallas_call(
        _ada_kernel,
        grid=(depth, n // tn),
        in_specs=[
            pl.BlockSpec((MOD_ROWS, d), lambda l, j: (0, 0)),
            pl.BlockSpec((1, d, tn), lambda l, j: (l, 0, j)),
            pl.BlockSpec((1, 1, tn), lambda l, j: (l, 0, j)),
        ],
        out_specs=pl.BlockSpec((1, MOD_ROWS, tn), lambda l, j: (l, 0, j)),
        out_shape=jax.ShapeDtypeStruct((depth, MOD_ROWS, n), F32),
        compiler_params=_params(2),
        name="ada_table",
    )(c_rows, ada_w, ada_b.reshape(depth, 1, n))


def _in_proj_kernel(*refs, joins, n_tiles):
    n_blk = ROW_TILE // SUB_ROWS
    x_refs, refs = refs[:n_blk], refs[n_blk:]
    ctx_ref = refs[0] if joins else None
    mod_ref, nw_ref, w_ref, q_ref, v_ref, zf_ref, zb_ref, g_ref = refs[joins:joins + 8]
    xs_ref = refs[-1] if joins else None
    b, i = pl.program_id(0), pl.program_id(1)
    d = D_MODEL
    blocks = _sub_blocks(i, n_tiles)
    hs = [None] * n_blk

    def pre(n):
        rows, is_ctx = blocks[n]
        x = x_refs[n][0]
        if joins:
            if is_ctx is not False:
                x = jnp.where(is_ctx, ctx_ref[0], x)
            xs_ref[0, rows] = x
        hs[n] = _prenorm(x, nw_ref[0:1, :], mod_ref, b, is_ctx, 0).astype(BF16)

    def project(n):
        rows, _ = blocks[n]
        h = hs[n]
        q_ref[0, rows] = (_dot(h, w_ref[:, 0:d]) * (HEAD_DIM ** -0.5)).astype(BF16)
        v_ref[0, rows] = _dot(h, w_ref[:, d:2 * d]).astype(BF16)
        zf_ref[0, rows] = _dot(h, w_ref[:, 2 * d:3 * d])
        zb_ref[0, rows] = _dot(h, w_ref[:, 3 * d:4 * d])
        g_ref[0, rows] = _dot(h, w_ref[:, 4 * d:5 * d]).astype(BF16)

    _emit_pipeline(n_blk, [(0, pre), (1, project)])


def _in_proj(x, ctx, mod_all, norm_w, w, layer):
    bsz, _, d = x.shape
    joins = ctx is not None
    s = x.shape[1] + (CTX_LEN if joins else 0)
    n_tiles = s // ROW_TILE
    n_blk = ROW_TILE // SUB_ROWS
    last_block = x.shape[1] // SUB_ROWS - 1
    tile = pl.BlockSpec((1, ROW_TILE, d), lambda b, i: (b, i, 0))

    def x_block(n):
        return pl.BlockSpec((1, SUB_ROWS, d), lambda b, i: (b, jnp.minimum(i * n_blk + n, last_block), 0))

    in_specs = [x_block(n) for n in range(n_blk)]
    in_specs += [pl.BlockSpec((1, CTX_LEN, d), lambda b, i: (b, 0, 0))] if joins else []
    in_specs += [_resident(mod_all, layer), _resident(norm_w, layer), _resident(w)]
    sds16 = jax.ShapeDtypeStruct((bsz, s, d), BF16)
    sds32 = jax.ShapeDtypeStruct((bsz, s, d), F32)
    out = pl.pallas_call(
        functools.partial(_in_proj_kernel, joins=joins, n_tiles=n_tiles),
        grid=(bsz, n_tiles),
        in_specs=in_specs,
        out_specs=[tile] * (6 if joins else 5),
        out_shape=[sds16, sds16, sds32, sds32, sds16] + ([sds32] if joins else []),
        compiler_params=_params(2),
        name="hgrn_in_proj",
    )(*([x] * n_blk), *([ctx] if joins else []), mod_all, norm_w, w)
    return (out[:5], out[5]) if joins else (out, x)


def _scan_levels():
    levels = []
    b = SCAN_CHUNK // 2
    while b >= SCAN_BASE:
        levels.append(b)
        b //= 2
    return levels


def _cumsum_matrix(forward):
    t = np.arange(SCAN_CHUNK)
    tri = t[None, :] <= t[:, None] if forward else t[None, :] >= t[:, None]
    return tri.astype(np.float32)


def _level_masks(forward):
    c = SCAN_CHUNK
    t = lax.broadcasted_iota(jnp.int32, (c, c), 0)
    s = lax.broadcasted_iota(jnp.int32, (c, c), 1)
    if not forward:
        t, s = (c - 1) - t, (c - 1) - s
    masks = []
    for b in _scan_levels():
        same = (t // (2 * b)) == (s // (2 * b))
        masks.append(same & (t % (2 * b) >= b) & (s % (2 * b) < b))
    same = (t // SCAN_BASE) == (s // SCAN_BASE)
    masks.append(same & (s <= t))
    return masks


def _pair_level(q, k, c, half, forward):
    packed_rows = 16
    qs, ks = [], []
    for start in range(0, SCAN_CHUNK, 2 * half):
        lo, hi = slice(start, start + half), slice(start + half, start + 2 * half)
        if forward:
            ref, q_rows, k_rows = c[start + half - 1:start + half], hi, lo
        else:
            ref, q_rows, k_rows = c[start + half:start + half + 1], lo, hi
        eq = jnp.exp2(c[q_rows] - ref)
        ek = jnp.exp2(ref - c[k_rows])
        if half % packed_rows == 0:
            zero = jnp.zeros((half, c.shape[-1]), BF16)
            qe, ke = (q[q_rows] * eq).astype(BF16), (k[k_rows] * ek).astype(BF16)
            qs += [zero, qe] if forward else [qe, zero]
            ks += [ke, zero] if forward else [zero, ke]
        else:
            e = jnp.concatenate([ek, eq] if forward else [eq, ek], axis=0)
            rows = slice(start, start + 2 * half)
            qs.append((q[rows] * e).astype(BF16))
            ks.append((k[rows] * e).astype(BF16))
    return jnp.concatenate(qs, axis=0), jnp.concatenate(ks, axis=0)


def _diag_level(q, k, c, forward):
    c3 = c.reshape(SCAN_CHUNK // SCAN_BASE, SCAN_BASE, c.shape[-1])
    r = SCAN_BASE // 2 - 1 if forward else SCAN_BASE // 2
    x = (c3 - c3[:, r:r + 1, :]).reshape(c.shape)
    return (q * jnp.exp2(x)).astype(BF16), (k * jnp.exp2(-x)).astype(BF16)


class _Unit:
    def __init__(self, q_ref, v_ref, z_ref, o_ref, st_ref, rows, h, tri, masks, forward):
        self.q_ref, self.v_ref, self.z_ref, self.o_ref, self.st_ref = q_ref, v_ref, z_ref, o_ref, st_ref
        self.rows, self.h, self.tri, self.masks, self.forward = rows, h, tri, masks, forward
        self.hs = slice(h * HEAD_DIM, (h + 1) * HEAD_DIM)


def _stage_gates(u, lb):
    lb = lb[:, u.hs]
    rest = 1.0 - lb
    z = u.z_ref[0, u.rows, u.hs]
    sig = 1.0 / (1.0 + jnp.exp2(z * (-1.0 / np.log(2.0))))
    gate = rest * sig
    g = jnp.log(jnp.maximum(lb + gate, F_FLOOR)) * (1.0 / np.log(2.0))
    u.k = rest - gate
    u.q = u.q_ref[0, u.rows, u.hs].astype(F32)
    u.v16 = u.v_ref[0, u.rows, u.hs]
    u.cs = _dot(u.tri, jnp.concatenate(_bf16_parts(g, CUMSUM_PARTS), axis=-1))


def _stage_levels(u):
    c = u.cs[:, :HEAD_DIM]
    for i in range(1, CUMSUM_PARTS):
        c = c + u.cs[:, i * HEAD_DIM:(i + 1) * HEAD_DIM]
    last = SCAN_CHUNK - 1 if u.forward else 0
    c_last = c[last:last + 1]
    u.q_in = (u.q * jnp.exp2(c)).astype(BF16)
    u.k_st = (u.k * jnp.exp2(c_last - c)).astype(BF16)
    u.e_last = jnp.exp2(c_last)
    u.levels = [_pair_level(u.q, u.k, c, b, u.forward) for b in _scan_levels()]
    u.levels.append(_diag_level(u.q, u.k, c, u.forward))


def _stage_scores(u):
    a = None
    for li, (qd, kd) in enumerate(u.levels):
        a_l = lax.dot_general(qd, kd, (((1,), (1,)), ((), ())), preferred_element_type=F32)
        a = a_l if li == 0 else jnp.where(u.masks[li], a_l, a)
    u.a16 = a.astype(BF16)


def _stage_output(u):
    contract_lanes = (((1,), (1,)), ((), ()))
    contract_rows = (((0,), (0,)), ((), ()))
    st = u.st_ref[u.h]
    o = _dot(u.a16, u.v16)
    o = o + lax.dot_general(u.q_in, st.astype(BF16), contract_lanes, preferred_element_type=F32)
    u.o_ref[0, u.rows, u.hs] = o.astype(BF16)
    u.st_ref[u.h] = st * u.e_last + lax.dot_general(u.v16, u.k_st, contract_rows,
                                                    preferred_element_type=F32)


def _scan_units(fwd_refs, bwd_refs, tri_f, tri_b):
    n_chunks = SCAN_BLOCK // SCAN_CHUNK
    masks = {True: _level_masks(True), False: _level_masks(False)}
    units = []
    for ci in range(n_chunks):
        for refs, tri, forward in ((fwd_refs, tri_f, True), (bwd_refs, tri_b, False)):
            c = ci if forward else n_chunks - 1 - ci
            rows = slice(c * SCAN_CHUNK, (c + 1) * SCAN_CHUNK)
            for h in range(HEADS):
                units.append(_Unit(*refs, rows, h, tri, masks[forward], forward))
    return units


def _scan_kernel(*refs, layer, n_casts):
    lb_ref, mf_ref, mb_ref, qf_ref, vf_ref, zf_ref, qb_ref, vb_ref, zb_ref = refs[:9]
    cast_in = refs[9:9 + n_casts]
    of_ref, ob_ref = refs[9 + n_casts:11 + n_casts]
    cast_out = refs[11 + n_casts:11 + 2 * n_casts]
    sf_ref, sb_ref = refs[11 + 2 * n_casts:]

    @pl.when(pl.program_id(1) == 0)
    def _():
        sf_ref[...] = jnp.zeros_like(sf_ref)
        sb_ref[...] = jnp.zeros_like(sb_ref)

    step = pl.program_id(0) * pl.num_programs(1) + pl.program_id(1)

    @pl.when(step % CAST_EVERY == 0)
    def _():
        for src_ref, dst_ref in zip(cast_in, cast_out):
            dst_ref[...] = src_ref[...].astype(BF16)

    raw = lb_ref[...]
    ex = jnp.exp(raw - jnp.max(raw, axis=0, keepdims=True))
    soft = ex / jnp.sum(ex, axis=0, keepdims=True)
    lb = jnp.zeros((1, D_MODEL), F32)
    for i in range(1, layer + 1):
        lb = lb + soft[i:i + 1]

    units = _scan_units((qf_ref, vf_ref, zf_ref, of_ref, sf_ref), (qb_ref, vb_ref, zb_ref, ob_ref, sb_ref),
                        mf_ref[...], mb_ref[...])
    stages = (functools.partial(_stage_gates, lb=lb), _stage_levels, _stage_scores, _stage_output)
    _emit_pipeline(len(units), [(off, lambda n, fn=fn: fn(units[n])) for off, fn in zip(SCAN_SKEW, stages)])


def _cast_block_rows(rows, n_blocks):
    packed_rows = 16
    for block in range(packed_rows, rows + 1, packed_rows):
        if rows % block == 0 and rows // block <= n_blocks:
            return block
    raise ValueError(f"cannot spread {rows} rows over {n_blocks} blocks")


def _scan(q, v, zf, zb, hgrn_lb, layer, casts):
    bsz, s, _ = q.shape
    c = SCAN_BLOCK
    n_steps = s // c
    n_ctx = CTX_LEN // c
    mf = jnp.asarray(_cumsum_matrix(True), BF16)
    mb = jnp.asarray(_cumsum_matrix(False), BF16)

    def fwd_block(i):
        return jnp.where(i < n_ctx, n_steps - n_ctx + i, i - n_ctx)

    def bwd_block(i):
        return n_steps - 1 - i

    def rows(block_of):
        return pl.BlockSpec((1, c, D_MODEL), lambda b, i: (b, block_of(i), 0))

    fwd, bwd = rows(fwd_block), rows(bwd_block)
    out_sds = jax.ShapeDtypeStruct((bsz, s, D_MODEL), BF16)

    cast_in, cast_out, cast_sds = [], [], []
    for w, row0, n_rows in casts:
        block = _cast_block_rows(n_rows, bsz * n_steps // CAST_EVERY)
        first, last = row0 // block, n_rows // block - 1

        def step_block(b, i, last=last):
            return jnp.minimum((b * n_steps + i) // CAST_EVERY, last)

        cast_in.append(pl.BlockSpec((block, w.shape[1]),
                                    lambda b, i, f=first, sb=step_block: (f + sb(b, i), 0)))
        cast_out.append(pl.BlockSpec((block, w.shape[1]), lambda b, i, sb=step_block: (sb(b, i), 0)))
        cast_sds.append(jax.ShapeDtypeStruct((n_rows, w.shape[1]), BF16))

    out = pl.pallas_call(
        functools.partial(_scan_kernel, layer=layer, n_casts=len(casts)),
        grid=(bsz, n_steps),
        in_specs=[_resident(hgrn_lb), _resident(mf), _resident(mb), fwd, fwd, fwd, bwd, bwd, bwd] + cast_in,
        out_specs=[fwd, bwd] + cast_out,
        out_shape=[out_sds, out_sds] + cast_sds,
        scratch_shapes=[pltpu.VMEM((HEADS, HEAD_DIM, HEAD_DIM), F32),
                        pltpu.VMEM((HEADS, HEAD_DIM, HEAD_DIM), F32)],
        compiler_params=_params(2),
        name="hgrn_scan",
    )(hgrn_lb, mf, mb, q, v, zf, q, v, zb, *[w for w, _, _ in casts])
    return out[0], out[1], out[2:]


def _readout_ffn_kernel(x_ref, of_ref, ob_ref, g_ref, mod_ref, nw_ref, gn_ref, w_ref, wi_ref, wo_ref,
                        o_ref, *, n_tiles):
    b, i = pl.program_id(0), pl.program_id(1)
    tail = _FfnTail(x_ref, o_ref, mod_ref, nw_ref, wi_ref, wo_ref, b, _sub_blocks(i, n_tiles))
    mixed = [None] * len(tail.blocks)

    def mix(r):
        rows, _ = tail.blocks[r]
        o = of_ref[0, rows].astype(F32) + ob_ref[0, rows].astype(F32)
        gn = gn_ref[...]
        heads = [_rms(o[:, h * HEAD_DIM:(h + 1) * HEAD_DIM]) * gn for h in range(HEADS)]
        y = jnp.concatenate(heads, axis=-1) * _silu(g_ref[0, rows].astype(F32))
        mixed[r] = y.astype(BF16)

    def project_out(r):
        tail.ys[r] = _dot(mixed[r], w_ref[...])

    tail.emit([mix, project_out])


def _readout_ffn(xs, o_f, o_b, g, mod_all, norm_w, gnorm_all, w_out_all, ffn_w_in, ffn_w_out,
                 layer, j, ffn_idx, out_rows):
    bsz, s, d = xs.shape
    n_tiles = s // ROW_TILE
    tile = pl.BlockSpec((1, ROW_TILE, d), lambda b, i: (b, i, 0))
    return pl.pallas_call(
        functools.partial(_readout_ffn_kernel, n_tiles=n_tiles),
        grid=(bsz, n_tiles),
        in_specs=[tile, tile, tile, tile, _resident(mod_all, layer), _resident(norm_w, layer),
                  _resident(gnorm_all, j), _resident(w_out_all, j),
                  _resident(ffn_w_in, ffn_idx), _resident(ffn_w_out, ffn_idx)],
        out_specs=tile,
        out_shape=jax.ShapeDtypeStruct((bsz, out_rows, d), F32),
        compiler_params=_params(2),
        name="hgrn_readout_ffn",
    )(xs, o_f, o_b, g, mod_all, norm_w, gnorm_all, w_out_all, ffn_w_in, ffn_w_out)


def _conv_ffn_kernel(x_ref, mod_ref, nw_ref, ci_ref, cw_ref, co_ref, wi_ref, wo_ref, o_ref,
                     *, n_tiles, seq):
    b, i = pl.program_id(0), pl.program_id(1)
    d = D_MODEL
    tail = _FfnTail(x_ref, o_ref, mod_ref, nw_ref, wi_ref, wo_ref, b, _sub_blocks(i, n_tiles))
    n_blk = len(tail.blocks)
    hs, us, gs, ts = ([None] * n_blk for _ in range(4))

    def pre(r):
        rows, is_ctx = tail.blocks[r]
        hs[r] = _prenorm(x_ref[0, rows], nw_ref[0:1, :], mod_ref, b, is_ctx, 0).astype(BF16)

    def project(r):
        gs[r] = _dot(hs[r], ci_ref[:, 0:d])
        us[r] = _dot(hs[r], ci_ref[:, d:2 * d]) * _dot(hs[r], ci_ref[:, 2 * d:3 * d])

    def convolve(r):
        rows, _ = tail.blocks[r]
        u = us[r]
        pos = i * ROW_TILE + rows.start + lax.broadcasted_iota(jnp.int32, (SUB_ROWS, 1), 0)
        in_ctx = pos >= seq
        has_prev = (pos != seq) & (in_ctx | (pos % GRID_W != 0))
        has_next = (pos != seq + CTX_LEN - 1) & (in_ctx | (pos % GRID_W != GRID_W - 1))
        u_prev = jnp.where(has_prev, pltpu.roll(u, 1, axis=0), 0.0)
        u_next = jnp.where(has_next, pltpu.roll(u, SUB_ROWS - 1, axis=0), 0.0)
        y = u_prev * cw_ref[0:1, :] + u * cw_ref[1:2, :] + u_next * cw_ref[2:3, :]
        ts[r] = (gs[r] * y).astype(BF16)

    def project_out(r):
        tail.ys[r] = _dot(ts[r], co_ref[...])

    tail.emit([pre, project, convolve, project_out])


def _conv_ffn(xs, mod_all, norm_w, w_in, conv_w_all, w_out_all, ffn_w_in, ffn_w_out,
              layer, j, ffn_idx, out_rows):
    bsz, s, d = xs.shape
    n_tiles = s // ROW_TILE
    tile = pl.BlockSpec((1, ROW_TILE, d), lambda b, i: (b, i, 0))
    return pl.pallas_call(
        functools.partial(_conv_ffn_kernel, n_tiles=n_tiles, seq=s - CTX_LEN),
        grid=(bsz, n_tiles),
        in_specs=[tile, _resident(mod_all, layer), _resident(norm_w, layer), _resident(w_in, 0),
                  _resident(conv_w_all, j), _resident(w_out_all, j),
                  _resident(ffn_w_in, ffn_idx), _resident(ffn_w_out, ffn_idx)],
        out_specs=tile,
        out_shape=jax.ShapeDtypeStruct((bsz, out_rows, d), F32),
        compiler_params=_params(2),
        name="conv_ffn",
    )(xs, mod_all, norm_w, w_in, conv_w_all, w_out_all, ffn_w_in, ffn_w_out)


class _FfnTail:
    def __init__(self, x_ref, o_ref, mod_ref, nw_ref, wi_ref, wo_ref, batch, blocks):
        self.x_ref, self.o_ref, self.mod_ref, self.nw_ref = x_ref, o_ref, mod_ref, nw_ref
        self.wi_ref, self.wo_ref, self.batch, self.blocks = wi_ref, wo_ref, batch, blocks
        self.hidden = wo_ref.shape[0]
        self.tf = self.hidden // FFN_SPLIT
        n_blk = len(blocks)
        self.ys, self.x1, self.hs, self.accs = ([None] * n_blk for _ in range(4))
        self.gates, self.ups, self.acts = ([None] * (n_blk * FFN_SPLIT) for _ in range(3))

    def residual(self, r):
        rows, is_ctx = self.blocks[r]
        x1 = _postnorm_residual(self.x_ref[0, rows], self.ys[r], self.nw_ref[1:2, :],
                                self.mod_ref, self.batch, is_ctx, 2)
        self.x1[r] = x1
        self.hs[r] = _prenorm(x1, self.nw_ref[2:3, :], self.mod_ref, self.batch, is_ctx, 3).astype(BF16)

    def project(self, n):
        r, f = divmod(n, FFN_SPLIT)
        lo, hi = f * self.tf, (f + 1) * self.tf
        self.gates[n] = _dot(self.hs[r], self.wi_ref[:, lo:hi])
        self.ups[n] = _dot(self.hs[r], self.wi_ref[:, self.hidden + lo:self.hidden + hi])

    def activate(self, n):
        self.acts[n] = (_silu(self.gates[n]) * self.ups[n]).astype(BF16)
        self.gates[n] = self.ups[n] = None

    def down(self, n):
        r, f = divmod(n, FFN_SPLIT)
        y = _dot(self.acts[n], self.wo_ref[f * self.tf:(f + 1) * self.tf, :])
        self.accs[r] = y if f == 0 else self.accs[r] + y
        self.acts[n] = None

    def finish(self, r):
        rows, is_ctx = self.blocks[r]
        self.o_ref[0, rows] = _postnorm_residual(self.x1[r], self.accs[r], self.nw_ref[3:4, :],
                                                 self.mod_ref, self.batch, is_ctx, 5)

    def emit(self, mixer_stages):
        def on_first(fn):
            return lambda n: fn(n // FFN_SPLIT) if n % FFN_SPLIT == 0 else None

        def on_last(fn):
            return lambda n: fn(n // FFN_SPLIT) if n % FFN_SPLIT == FFN_SPLIT - 1 else None

        stages = [on_first(fn) for fn in mixer_stages]
        stages += [on_first(self.residual), self.project, self.activate, self.down, on_last(self.finish)]
        _emit_pipeline(len(self.blocks) * FFN_SPLIT, list(enumerate(stages)))


def kernel(x, c, ctx, c_ctx, ada_w, ada_b, norm_w, hgrn_w_in, hgrn_w_out, hgrn_gnorm, hgrn_lb,
           conv_w_in, conv_w, conv_w_out, ffn_w_in, ffn_w_out):
    bsz, seq, d = x.shape
    depth = ada_w.shape[0]
    s_all = seq + CTX_LEN
    assert d == D_MODEL and ctx.shape[1] == CTX_LEN and bsz < MOD_ROWS and depth % 2 == 0
    assert s_all % ROW_TILE == 0 and ROW_TILE % SUB_ROWS == 0 and SUB_ROWS % GRID_W == 0
    assert seq % SCAN_BLOCK == 0 and CTX_LEN % SCAN_BLOCK == 0 and seq % GRID_W == 0

    c_rows = jnp.zeros((MOD_ROWS, d), F32).at[:bsz].set(c).at[MOD_ROWS - 1].set(c_ctx)
    mod_all = _ada_table(c_rows, ada_w, ada_b)

    d_ffn = ffn_w_in.shape[2]
    hidden = ffn_w_out.shape[1]
    lanes = 128
    assert hidden % (FFN_SPLIT * lanes) == 0 and d_ffn == 2 * hidden
    n_rec = hgrn_w_in.shape[0]
    w_in_first = hgrn_w_in[0].astype(BF16)
    hgrn_w_out, conv_w_out = hgrn_w_out.astype(BF16), conv_w_out.astype(BF16)
    hgrn_w_in_rows = hgrn_w_in.reshape(n_rec * d, -1)
    conv_w_in_rows = conv_w_in.reshape(-1, conv_w_in.shape[2])
    ffn_w_in_rows = ffn_w_in.reshape(depth * d, d_ffn)
    ffn_w_out_rows = ffn_w_out.reshape(depth * hidden, d)
    gnorm = hgrn_gnorm.reshape(-1, 1, HEAD_DIM)

    xs, w_in_next = None, w_in_first
    for j in range(n_rec):
        l = 2 * j
        first = xs is None
        (q, v, zf, zb, g), xs = _in_proj(x if first else xs, ctx if first else None,
                                         mod_all, norm_w, w_in_next, l)
        casts = [(ffn_w_in_rows, l * d, 2 * d), (ffn_w_out_rows, l * hidden, 2 * hidden),
                 (conv_w_in_rows, j * d, d)]
        if j + 1 < n_rec:
            casts.append((hgrn_w_in_rows, (j + 1) * d, d))
        o_f, o_b, cast = _scan(q, v, zf, zb, hgrn_lb, j, casts)
        ffn_in = cast[0].reshape(2, d, d_ffn)
        ffn_out = cast[1].reshape(2, hidden, d)
        conv_in = cast[2].reshape(1, d, -1)
        w_in_next = cast[3] if j + 1 < n_rec else None
        xs = _readout_ffn(xs, o_f, o_b, g, mod_all, norm_w, gnorm, hgrn_w_out, ffn_in, ffn_out,
                          l, j, 0, s_all)
        xs = _conv_ffn(xs, mod_all, norm_w, conv_in, conv_w, conv_w_out, ffn_in, ffn_out,
                       l + 1, j, 1, seq if l + 1 == depth - 1 else s_all)
    return xs
```

```python
import functools

import numpy as np
import jax
import jax.numpy as jnp
from jax import lax
from jax.experimental import pallas as pl
from jax.experimental.pallas import tpu as pltpu

D_MODEL = 1024
CTX_LEN = 256
GRID_W = 64
HEADS = 8
HEAD_DIM = D_MODEL // HEADS
CONV_WIDTH = 3
N_MOD = 6
EPS = 1e-6
F_FLOOR = 1e-6

SCAN_BLOCK = 256
SCAN_CHUNK = 128
SCAN_BASE = 8
SCAN_SKEW = (0, 1, 2, 3)
CAST_EVERY = 4
CUMSUM_PARTS = 2
ROW_TILE = 768
SUB_ROWS = CTX_LEN
FFN_SPLIT = 11
MOD_ROWS = 8
VMEM_LIMIT = 56 * 1024 * 1024

F32 = jnp.float32
BF16 = jnp.bfloat16


def _bf16_parts(x, n):
    parts = []
    r = x
    for i in range(n):
        p = r.astype(BF16)
        parts.append(p)
        if i + 1 < n:
            r = r - p.astype(F32)
    return parts


def _dot(a, b):
    return jnp.dot(a, b, preferred_element_type=F32)


def _rms(x):
    return x * lax.rsqrt(jnp.mean(x * x, axis=-1, keepdims=True) + EPS)


def _silu(x):
    return x * (1.0 / (1.0 + jnp.exp(-x)))


def _emit_pipeline(n_items, stages):
    depth = max(off for off, _ in stages)
    for t in range(n_items + depth):
        for off, fn in stages:
            if 0 <= t - off < n_items:
                fn(t - off)


def _sub_blocks(tile_idx, n_tiles):
    n = ROW_TILE // SUB_ROWS
    blocks = []
    for r in range(n):
        rows = slice(r * SUB_ROWS, (r + 1) * SUB_ROWS)
        blocks.append((rows, (tile_idx == n_tiles - 1) if r == n - 1 else False))
    return blocks


def _mod_vec(mod_ref, batch, is_ctx, idx):
    row = batch if is_ctx is False else jnp.where(is_ctx, MOD_ROWS - 1, batch)
    return mod_ref[pl.ds(row, 1), idx * D_MODEL:(idx + 1) * D_MODEL]


def _prenorm(x, nw, mod_ref, batch, is_ctx, shift_idx):
    shift = _mod_vec(mod_ref, batch, is_ctx, shift_idx)
    scale = _mod_vec(mod_ref, batch, is_ctx, shift_idx + 1)
    return _rms(x) * (nw * (1.0 + scale)) + shift


def _postnorm_residual(x, y, nw, mod_ref, batch, is_ctx, gate_idx):
    gate = _mod_vec(mod_ref, batch, is_ctx, gate_idx)
    return x + _rms(y) * (gate * nw)


def _params(n_axes):
    return pltpu.CompilerParams(dimension_semantics=("arbitrary",) * n_axes,
                                vmem_limit_bytes=VMEM_LIMIT)


def _resident(a, layer=None):
    if layer is None:
        shape, index = a.shape, (0,) * a.ndim
    else:
        shape, index = (None,) + a.shape[1:], (layer,) + (0,) * (a.ndim - 1)
    return pl.BlockSpec(shape, lambda *_: index, pipeline_mode=pl.Buffered(1))


def _ada_kernel(c_ref, w_ref, b_ref, o_ref):
    a = _silu(c_ref[...])
    a_hi = a.astype(BF16)
    a_hi32 = a_hi.astype(F32)
    a_parts = jnp.concatenate([a_hi32, a - a_hi32], axis=0).astype(BF16)
    w_hi, w_lo = _bf16_parts(w_ref[0], 2)
    both = _dot(a_parts, w_hi)
    o_ref[0] = both[:MOD_ROWS] + both[MOD_ROWS:] + _dot(a_hi, w_lo) + b_ref[0]


def _ada_table(c_rows, ada_w, ada_b):
    depth, d, n = ada_w.shape
    tn = n // 2
    return pl.pallas_call(
        _ada_kernel,
        grid=(depth, n // tn),
        in_specs=[
            pl.BlockSpec((MOD_ROWS, d), lambda l, j: (0, 0)),
            pl.BlockSpec((1, d, tn), lambda l, j: (l, 0, j)),
            pl.BlockSpec((1, 1, tn), lambda l, j: (l, 0, j)),
        ],
        out_specs=pl.BlockSpec((1, MOD_ROWS, tn), lambda l, j: (l, 0, j)),
        out_shape=jax.ShapeDtypeStruct((depth, MOD_ROWS, n), F32),
        compiler_params=_params(2),
        name="ada_table",
    )(c_rows, ada_w, ada_b.reshape(depth, 1, n))


def _in_proj_kernel(*refs, joins, n_tiles):
    n_blk = ROW_TILE // SUB_ROWS
    x_refs, refs = refs[:n_blk], refs[n_blk:]
    ctx_ref = refs[0] if joins else None
    mod_ref, nw_ref, w_ref, q_ref, v_ref, zf_ref, zb_ref, g_ref = refs[joins:joins + 8]
    xs_ref = refs[-1] if joins else None
    b, i = pl.program_id(0), pl.program_id(1)
    d = D_MODEL
    blocks = _sub_blocks(i, n_tiles)
    hs = [None] * n_blk

    def pre(n):
        rows, is_ctx = blocks[n]
        x = x_refs[n][0]
        if joins:
            if is_ctx is not False:
                x = jnp.where(is_ctx, ctx_ref[0], x)
            xs_ref[0, rows] = x
        hs[n] = _prenorm(x, nw_ref[0:1, :], mod_ref, b, is_ctx, 0).astype(BF16)

    def project(n):
        rows, _ = blocks[n]
        h = hs[n]
        q_ref[0, rows] = (_dot(h, w_ref[:, 0:d]) * (HEAD_DIM ** -0.5)).astype(BF16)
        v_ref[0, rows] = _dot(h, w_ref[:, d:2 * d]).astype(BF16)
        zf_ref[0, rows] = _dot(h, w_ref[:, 2 * d:3 * d])
        zb_ref[0, rows] = _dot(h, w_ref[:, 3 * d:4 * d])
        g_ref[0, rows] = _dot(h, w_ref[:, 4 * d:5 * d]).astype(BF16)

    _emit_pipeline(n_blk, [(0, pre), (1, project)])


def _in_proj(x, ctx, mod_all, norm_w, w, layer):
    bsz, _, d = x.shape
    joins = ctx is not None
    s = x.shape[1] + (CTX_LEN if joins else 0)
    n_tiles = s // ROW_TILE
    n_blk = ROW_TILE // SUB_ROWS
    last_block = x.shape[1] // SUB_ROWS - 1
    tile = pl.BlockSpec((1, ROW_TILE, d), lambda b, i: (b, i, 0))

    def x_block(n):
        return pl.BlockSpec((1, SUB_ROWS, d), lambda b, i: (b, jnp.minimum(i * n_blk + n, last_block), 0))

    in_specs = [x_block(n) for n in range(n_blk)]
    in_specs += [pl.BlockSpec((1, CTX_LEN, d), lambda b, i: (b, 0, 0))] if joins else []
    in_specs += [_resident(mod_all, layer), _resident(norm_w, layer), _resident(w)]
    sds16 = jax.ShapeDtypeStruct((bsz, s, d), BF16)
    sds32 = jax.ShapeDtypeStruct((bsz, s, d), F32)
    out = pl.pallas_call(
        functools.partial(_in_proj_kernel, joins=joins, n_tiles=n_tiles),
        grid=(bsz, n_tiles),
        in_specs=in_specs,
        out_specs=[tile] * (6 if joins else 5),
        out_shape=[sds16, sds16, sds32, sds32, sds16] + ([sds32] if joins else []),
        compiler_params=_params(2),
        name="hgrn_in_proj",
    )(*([x] * n_blk), *([ctx] if joins else []), mod_all, norm_w, w)
    return (out[:5], out[5]) if joins else (out, x)


def _scan_levels():
    levels = []
    b = SCAN_CHUNK // 2
    while b >= SCAN_BASE:
        levels.append(b)
        b //= 2
    return levels


def _cumsum_matrix(forward):
    t = np.arange(SCAN_CHUNK)
    tri = t[None, :] <= t[:, None] if forward else t[None, :] >= t[:, None]
    return tri.astype(np.float32)


def _level_masks(forward):
    c = SCAN_CHUNK
    t = lax.broadcasted_iota(jnp.int32, (c, c), 0)
    s = lax.broadcasted_iota(jnp.int32, (c, c), 1)
    if not forward:
        t, s = (c - 1) - t, (c - 1) - s
    masks = []
    for b in _scan_levels():
        same = (t // (2 * b)) == (s // (2 * b))
        masks.append(same & (t % (2 * b) >= b) & (s % (2 * b) < b))
    same = (t // SCAN_BASE) == (s // SCAN_BASE)
    masks.append(same & (s <= t))
    return masks


def _pair_level(q, k, c, half, forward):
    packed_rows = 16
    qs, ks = [], []
    for start in range(0, SCAN_CHUNK, 2 * half):
        lo, hi = slice(start, start + half), slice(start + half, start + 2 * half)
        if forward:
            ref, q_rows, k_rows = c[start + half - 1:start + half], hi, lo
        else:
            ref, q_rows, k_rows = c[start + half:start + half + 1], lo, hi
        eq = jnp.exp2(c[q_rows] - ref)
        ek = jnp.exp2(ref - c[k_rows])
        if half % packed_rows == 0:
            zero = jnp.zeros((half, c.shape[-1]), BF16)
            qe, ke = (q[q_rows] * eq).astype(BF16), (k[k_rows] * ek).astype(BF16)
            qs += [zero, qe] if forward else [qe, zero]
            ks += [ke, zero] if forward else [zero, ke]
        else:
            e = jnp.concatenate([ek, eq] if forward else [eq, ek], axis=0)
            rows = slice(start, start + 2 * half)
            qs.append((q[rows] * e).astype(BF16))
            ks.append((k[rows] * e).astype(BF16))
    return jnp.concatenate(qs, axis=0), jnp.concatenate(ks, axis=0)


def _diag_level(q, k, c, forward):
    c3 = c.reshape(SCAN_CHUNK // SCAN_BASE, SCAN_BASE, c.shape[-1])
    r = SCAN_BASE // 2 - 1 if forward else SCAN_BASE // 2
    x = (c3 - c3[:, r:r + 1, :]).reshape(c.shape)
    return (q * jnp.exp2(x)).astype(BF16), (k * jnp.exp2(-x)).astype(BF16)


class _Unit:
    def __init__(self, q_ref, v_ref, z_ref, o_ref, st_ref, rows, h, tri, masks, forward):
        self.q_ref, self.v_ref, self.z_ref, self.o_ref, self.st_ref = q_ref, v_ref, z_ref, o_ref, st_ref
        self.rows, self.h, self.tri, self.masks, self.forward = rows, h, tri, masks, forward
        self.hs = slice(h * HEAD_DIM, (h + 1) * HEAD_DIM)


def _stage_gates(u, lb):
    lb = lb[:, u.hs]
    rest = 1.0 - lb
    z = u.z_ref[0, u.rows, u.hs]
    sig = 1.0 / (1.0 + jnp.exp2(z * (-1.0 / np.log(2.0))))
    gate = rest * sig
    g = jnp.log(jnp.maximum(lb + gate, F_FLOOR)) * (1.0 / np.log(2.0))
    u.k = rest - gate
    u.q = u.q_ref[0, u.rows, u.hs].astype(F32)
    u.v16 = u.v_ref[0, u.rows, u.hs]
    u.cs = _dot(u.tri, jnp.concatenate(_bf16_parts(g, CUMSUM_PARTS), axis=-1))


def _stage_levels(u):
    c = u.cs[:, :HEAD_DIM]
    for i in range(1, CUMSUM_PARTS):
        c = c + u.cs[:, i * HEAD_DIM:(i + 1) * HEAD_DIM]
    last = SCAN_CHUNK - 1 if u.forward else 0
    c_last = c[last:last + 1]
    u.q_in = (u.q * jnp.exp2(c)).astype(BF16)
    u.k_st = (u.k * jnp.exp2(c_last - c)).astype(BF16)
    u.e_last = jnp.exp2(c_last)
    u.levels = [_pair_level(u.q, u.k, c, b, u.forward) for b in _scan_levels()]
    u.levels.append(_diag_level(u.q, u.k, c, u.forward))


def _stage_scores(u):
    a = None
    for li, (qd, kd) in enumerate(u.levels):
        a_l = lax.dot_general(qd, kd, (((1,), (1,)), ((), ())), preferred_element_type=F32)
        a = a_l if li == 0 else jnp.where(u.masks[li], a_l, a)
    u.a16 = a.astype(BF16)


def _stage_output(u):
    contract_lanes = (((1,), (1,)), ((), ()))
    contract_rows = (((0,), (0,)), ((), ()))
    st = u.st_ref[u.h]
    o = _dot(u.a16, u.v16)
    o = o + lax.dot_general(u.q_in, st.astype(BF16), contract_lanes, preferred_element_type=F32)
    u.o_ref[0, u.rows, u.hs] = o.astype(BF16)
    u.st_ref[u.h] = st * u.e_last + lax.dot_general(u.v16, u.k_st, contract_rows,
                                                    preferred_element_type=F32)


def _scan_units(fwd_refs, bwd_refs, tri_f, tri_b):
    n_chunks = SCAN_BLOCK // SCAN_CHUNK
    masks = {True: _level_masks(True), False: _level_masks(False)}
    units = []
    for ci in range(n_chunks):
        for refs, tri, forward in ((fwd_refs, tri_f, True), (bwd_refs, tri_b, False)):
            c = ci if forward else n_chunks - 1 - ci
            rows = slice(c * SCAN_CHUNK, (c + 1) * SCAN_CHUNK)
            for h in range(HEADS):
                units.append(_Unit(*refs, rows, h, tri, masks[forward], forward))
    return units


def _scan_kernel(*refs, layer, n_casts):
    lb_ref, mf_ref, mb_ref, qf_ref, vf_ref, zf_ref, qb_ref, vb_ref, zb_ref = refs[:9]
    cast_in = refs[9:9 + n_casts]
    of_ref, ob_ref = refs[9 + n_casts:11 + n_casts]
    cast_out = refs[11 + n_casts:11 + 2 * n_casts]
    sf_ref, sb_ref = refs[11 + 2 * n_casts:]

    @pl.when(pl.program_id(1) == 0)
    def _():
        sf_ref[...] = jnp.zeros_like(sf_ref)
        sb_ref[...] = jnp.zeros_like(sb_ref)

    step = pl.program_id(0) * pl.num_programs(1) + pl.program_id(1)

    @pl.when(step % CAST_EVERY == 0)
    def _():
        for src_ref, dst_ref in zip(cast_in, cast_out):
            dst_ref[...] = src_ref[...].astype(BF16)

    raw = lb_ref[...]
    ex = jnp.exp(raw - jnp.max(raw, axis=0, keepdims=True))
    soft = ex / jnp.sum(ex, axis=0, keepdims=True)
    lb = jnp.zeros((1, D_MODEL), F32)
    for i in range(1, layer + 1):
        lb = lb + soft[i:i + 1]

    units = _scan_units((qf_ref, vf_ref, zf_ref, of_ref, sf_ref), (qb_ref, vb_ref, zb_ref, ob_ref, sb_ref),
                        mf_ref[...], mb_ref[...])
    stages = (functools.partial(_stage_gates, lb=lb), _stage_levels, _stage_scores, _stage_output)
    _emit_pipeline(len(units), [(off, lambda n, fn=fn: fn(units[n])) for off, fn in zip(SCAN_SKEW, stages)])


def _cast_block_rows(rows, n_blocks):
    packed_rows = 16
    for block in range(packed_rows, rows + 1, packed_rows):
        if rows % block == 0 and rows // block <= n_blocks:
            return block
    raise ValueError(f"cannot spread {rows} rows over {n_blocks} blocks")


def _scan(q, v, zf, zb, hgrn_lb, layer, casts):
    bsz, s, _ = q.shape
    c = SCAN_BLOCK
    n_steps = s // c
    n_ctx = CTX_LEN // c
    mf = jnp.asarray(_cumsum_matrix(True), BF16)
    mb = jnp.asarray(_cumsum_matrix(False), BF16)

    def fwd_block(i):
        return jnp.where(i < n_ctx, n_steps - n_ctx + i, i - n_ctx)

    def bwd_block(i):
        return n_steps - 1 - i

    def rows(block_of):
        return pl.BlockSpec((1, c, D_MODEL), lambda b, i: (b, block_of(i), 0))

    fwd, bwd = rows(fwd_block), rows(bwd_block)
    out_sds = jax.ShapeDtypeStruct((bsz, s, D_MODEL), BF16)

    cast_in, cast_out, cast_sds = [], [], []
    for w, row0, n_rows in casts:
        block = _cast_block_rows(n_rows, bsz * n_steps // CAST_EVERY)
        first, last = row0 // block, n_rows // block - 1

        def step_block(b, i, last=last):
            return jnp.minimum((b * n_steps + i) // CAST_EVERY, last)

        cast_in.append(pl.BlockSpec((block, w.shape[1]),
                                    lambda b, i, f=first, sb=step_block: (f + sb(b, i), 0)))
        cast_out.append(pl.BlockSpec((block, w.shape[1]), lambda b, i, sb=step_block: (sb(b, i), 0)))
        cast_sds.append(jax.ShapeDtypeStruct((n_rows, w.shape[1]), BF16))

    out = pl.pallas_call(
        functools.partial(_scan_kernel, layer=layer, n_casts=len(casts)),
        grid=(bsz, n_steps),
        in_specs=[_resident(hgrn_lb), _resident(mf), _resident(mb), fwd, fwd, fwd, bwd, bwd, bwd] + cast_in,
        out_specs=[fwd, bwd] + cast_out,
        out_shape=[out_sds, out_sds] + cast_sds,
        scratch_shapes=[pltpu.VMEM((HEADS, HEAD_DIM, HEAD_DIM), F32),
                        pltpu.VMEM((HEADS, HEAD_DIM, HEAD_DIM), F32)],
        compiler_params=_params(2),
        name="hgrn_scan",
    )(hgrn_lb, mf, mb, q, v, zf, q, v, zb, *[w for w, _, _ in casts])
    return out[0], out[1], out[2:]


def _readout_ffn_kernel(x_ref, of_ref, ob_ref, g_ref, mod_ref, nw_ref, gn_ref, w_ref, wi_ref, wo_ref,
                        o_ref, *, n_tiles):
    b, i = pl.program_id(0), pl.program_id(1)
    tail = _FfnTail(x_ref, o_ref, mod_ref, nw_ref, wi_ref, wo_ref, b, _sub_blocks(i, n_tiles))
    mixed = [None] * len(tail.blocks)

    def mix(r):
        rows, _ = tail.blocks[r]
        o = of_ref[0, rows].astype(F32) + ob_ref[0, rows].astype(F32)
        gn = gn_ref[...]
        heads = [_rms(o[:, h * HEAD_DIM:(h + 1) * HEAD_DIM]) * gn for h in range(HEADS)]
        y = jnp.concatenate(heads, axis=-1) * _silu(g_ref[0, rows].astype(F32))
        mixed[r] = y.astype(BF16)

    def project_out(r):
        tail.ys[r] = _dot(mixed[r], w_ref[...])

    tail.emit([mix, project_out])


def _readout_ffn(xs, o_f, o_b, g, mod_all, norm_w, gnorm_all, w_out_all, ffn_w_in, ffn_w_out,
                 layer, j, ffn_idx, out_rows):
    bsz, s, d = xs.shape
    n_tiles = s // ROW_TILE
    tile = pl.BlockSpec((1, ROW_TILE, d), lambda b, i: (b, i, 0))
    return pl.pallas_call(
        functools.partial(_readout_ffn_kernel, n_tiles=n_tiles),
        grid=(bsz, n_tiles),
        in_specs=[tile, tile, tile, tile, _resident(mod_all, layer), _resident(norm_w, layer),
                  _resident(gnorm_all, j), _resident(w_out_all, j),
                  _resident(ffn_w_in, ffn_idx), _resident(ffn_w_out, ffn_idx)],
        out_specs=tile,
        out_shape=jax.ShapeDtypeStruct((bsz, out_rows, d), F32),
        compiler_params=_params(2),
        name="hgrn_readout_ffn",
    )(xs, o_f, o_b, g, mod_all, norm_w, gnorm_all, w_out_all, ffn_w_in, ffn_w_out)


def _conv_ffn_kernel(x_ref, mod_ref, nw_ref, ci_ref, cw_ref, co_ref, wi_ref, wo_ref, o_ref,
                     *, n_tiles, seq):
    b, i = pl.program_id(0), pl.program_id(1)
    d = D_MODEL
    tail = _FfnTail(x_ref, o_ref, mod_ref, nw_ref, wi_ref, wo_ref, b, _sub_blocks(i, n_tiles))
    n_blk = len(tail.blocks)
    hs, us, gs, ts = ([None] * n_blk for _ in range(4))

    def pre(r):
        rows, is_ctx = tail.blocks[r]
        hs[r] = _prenorm(x_ref[0, rows], nw_ref[0:1, :], mod_ref, b, is_ctx, 0).astype(BF16)

    def project(r):
        gs[r] = _dot(hs[r], ci_ref[:, 0:d])
        us[r] = _dot(hs[r], ci_ref[:, d:2 * d]) * _dot(hs[r], ci_ref[:, 2 * d:3 * d])

    def convolve(r):
        rows, _ = tail.blocks[r]
        u = us[r]
        pos = i * ROW_TILE + rows.start + lax.broadcasted_iota(jnp.int32, (SUB_ROWS, 1), 0)
        in_ctx = pos >= seq
        has_prev = (pos != seq) & (in_ctx | (pos % GRID_W != 0))
        has_next = (pos != seq + CTX_LEN - 1) & (in_ctx | (pos % GRID_W != GRID_W - 1))
        u_prev = jnp.where(has_prev, pltpu.roll(u, 1, axis=0), 0.0)
        u_next = jnp.where(has_next, pltpu.roll(u, SUB_ROWS - 1, axis=0), 0.0)
        y = u_prev * cw_ref[0:1, :] + u * cw_ref[1:2, :] + u_next * cw_ref[2:3, :]
        ts[r] = (gs[r] * y).astype(BF16)

    def project_out(r):
        tail.ys[r] = _dot(ts[r], co_ref[...])

    tail.emit([pre, project, convolve, project_out])


def _conv_ffn(xs, mod_all, norm_w, w_in, conv_w_all, w_out_all, ffn_w_in, ffn_w_out,
              layer, j, ffn_idx, out_rows):
    bsz, s, d = xs.shape
    n_tiles = s // ROW_TILE
    tile = pl.BlockSpec((1, ROW_TILE, d), lambda b, i: (b, i, 0))
    return pl.pallas_call(
        functools.partial(_conv_ffn_kernel, n_tiles=n_tiles, seq=s - CTX_LEN),
        grid=(bsz, n_tiles),
        in_specs=[tile, _resident(mod_all, layer), _resident(norm_w, layer), _resident(w_in, 0),
                  _resident(conv_w_all, j), _resident(w_out_all, j),
                  _resident(ffn_w_in, ffn_idx), _resident(ffn_w_out, ffn_idx)],
        out_specs=tile,
        out_shape=jax.ShapeDtypeStruct((bsz, out_rows, d), F32),
        compiler_params=_params(2),
        name="conv_ffn",
    )(xs, mod_all, norm_w, w_in, conv_w_all, w_out_all, ffn_w_in, ffn_w_out)


class _FfnTail:
    def __init__(self, x_ref, o_ref, mod_ref, nw_ref, wi_ref, wo_ref, batch, blocks):
        self.x_ref, self.o_ref, self.mod_ref, self.nw_ref = x_ref, o_ref, mod_ref, nw_ref
        self.wi_ref, self.wo_ref, self.batch, self.blocks = wi_ref, wo_ref, batch, blocks
        self.hidden = wo_ref.shape[0]
        self.tf = self.hidden // FFN_SPLIT
        n_blk = len(blocks)
        self.ys, self.x1, self.hs, self.accs = ([None] * n_blk for _ in range(4))
        self.gates, self.ups, self.acts = ([None] * (n_blk * FFN_SPLIT) for _ in range(3))

    def residual(self, r):
        rows, is_ctx = self.blocks[r]
        x1 = _postnorm_residual(self.x_ref[0, rows], self.ys[r], self.nw_ref[1:2, :],
                                self.mod_ref, self.batch, is_ctx, 2)
        self.x1[r] = x1
        self.hs[r] = _prenorm(x1, self.nw_ref[2:3, :], self.mod_ref, self.batch, is_ctx, 3).astype(BF16)

    def project(self, n):
        r, f = divmod(n, FFN_SPLIT)
        lo, hi = f * self.tf, (f + 1) * self.tf
        self.gates[n] = _dot(self.hs[r], self.wi_ref[:, lo:hi])
        self.ups[n] = _dot(self.hs[r], self.wi_ref[:, self.hidden + lo:self.hidden + hi])

    def activate(self, n):
        self.acts[n] = (_silu(self.gates[n]) * self.ups[n]).astype(BF16)
        self.gates[n] = self.ups[n] = None

    def down(self, n):
        r, f = divmod(n, FFN_SPLIT)
        y = _dot(self.acts[n], self.wo_ref[f * self.tf:(f + 1) * self.tf, :])
        self.accs[r] = y if f == 0 else self.accs[r] + y
        self.acts[n] = None

    def finish(self, r):
        rows, is_ctx = self.blocks[r]
        self.o_ref[0, rows] = _postnorm_residual(self.x1[r], self.accs[r], self.nw_ref[3:4, :],
                                                 self.mod_ref, self.batch, is_ctx, 5)

    def emit(self, mixer_stages):
        def on_first(fn):
            return lambda n: fn(n // FFN_SPLIT) if n % FFN_SPLIT == 0 else None

        def on_last(fn):
            return lambda n: fn(n // FFN_SPLIT) if n % FFN_SPLIT == FFN_SPLIT - 1 else None

        stages = [on_first(fn) for fn in mixer_stages]
        stages += [on_first(self.residual), self.project, self.activate, self.down, on_last(self.finish)]
        _emit_pipeline(len(self.blocks) * FFN_SPLIT, list(enumerate(stages)))


def kernel(x, c, ctx, c_ctx, ada_w, ada_b, norm_w, hgrn_w_in, hgrn_w_out, hgrn_gnorm, hgrn_lb,
           conv_w_in, conv_w, conv_w_out, ffn_w_in, ffn_w_out):
    bsz, seq, d = x.shape
    depth = ada_w.shape[0]
    s_all = seq + CTX_LEN
    assert d == D_MODEL and ctx.shape[1] == CTX_LEN and bsz < MOD_ROWS and depth % 2 == 0
    assert s_all % ROW_TILE == 0 and ROW_TILE % SUB_ROWS == 0 and SUB_ROWS % GRID_W == 0
    assert seq % SCAN_BLOCK == 0 and CTX_LEN % SCAN_BLOCK == 0 and seq % GRID_W == 0

    c_rows = jnp.zeros((MOD_ROWS, d), F32).at[:bsz].set(c).at[MOD_ROWS - 1].set(c_ctx)
    mod_all = _ada_table(c_rows, ada_w, ada_b)

    d_ffn = ffn_w_in.shape[2]
    hidden = ffn_w_out.shape[1]
    lanes = 128
    assert hidden % (FFN_SPLIT * lanes) == 0 and d_ffn == 2 * hidden
    n_rec = hgrn_w_in.shape[0]
    w_in_first = hgrn_w_in[0].astype(BF16)
    hgrn_w_out, conv_w_out = hgrn_w_out.astype(BF16), conv_w_out.astype(BF16)
    hgrn_w_in_rows = hgrn_w_in.reshape(n_rec * d, -1)
    conv_w_in_rows = conv_w_in.reshape(-1, conv_w_in.shape[2])
    ffn_w_in_rows = ffn_w_in.reshape(depth * d, d_ffn)
    ffn_w_out_rows = ffn_w_out.reshape(depth * hidden, d)
    gnorm = hgrn_gnorm.reshape(-1, 1, HEAD_DIM)

    xs, w_in_next = None, w_in_first
    for j in range(n_rec):
        l = 2 * j
        first = xs is None
        (q, v, zf, zb, g), xs = _in_proj(x if first else xs, ctx if first else None,
                                         mod_all, norm_w, w_in_next, l)
        casts = [(ffn_w_in_rows, l * d, 2 * d), (ffn_w_out_rows, l * hidden, 2 * hidden),
                 (conv_w_in_rows, j * d, d)]
        if j + 1 < n_rec:
            casts.append((hgrn_w_in_rows, (j + 1) * d, d))
        o_f, o_b, cast = _scan(q, v, zf, zb, hgrn_lb, j, casts)
        ffn_in = cast[0].reshape(2, d, d_ffn)
        ffn_out = cast[1].reshape(2, hidden, d)
        conv_in = cast[2].reshape(1, d, -1)
        w_in_next = cast[3] if j + 1 < n_rec else None
        xs = _readout_ffn(xs, o_f, o_b, g, mod_all, norm_w, gnorm, hgrn_w_out, ffn_in, ffn_out,
                          l, j, 0, s_all)
        xs = _conv_ffn(xs, mod_all, norm_w, conv_in, conv_w, conv_w_out, ffn_in, ffn_out,
                       l + 1, j, 1, seq if l + 1 == depth - 1 else s_all)
    return xs
```

```python
import functools

import numpy as np
import jax
import jax.numpy as jnp
from jax import lax
from jax.experimental import pallas as pl
from jax.experimental.pallas import tpu as pltpu

D_MODEL = 1024
CTX_LEN = 256
GRID_W = 64
HEADS = 8
HEAD_DIM = D_MODEL // HEADS
EPS = 1e-6
F_FLOOR = 1e-6

SCAN_BLOCK = 256
SCAN_CHUNK = 128
SCAN_BASE = 8
SCAN_SKEW = (0, 1, 2, 3)
CAST_EVERY = 4
CUMSUM_PARTS = 2
ROW_TILE = 768
SUB_ROWS = CTX_LEN
FFN_SPLIT = 11
MOD_ROWS = 8
VMEM_LIMIT = 56 * 1024 * 1024

F32 = jnp.float32
BF16 = jnp.bfloat16


def _bf16_parts(x, n):
    parts = []
    r = x
    for i in range(n):
        p = r.astype(BF16)
        parts.append(p)
        if i + 1 < n:
            r = r - p.astype(F32)
    return parts


def _dot(a, b):
    return jnp.dot(a, b, preferred_element_type=F32)


def _rms(x):
    return x * lax.rsqrt(jnp.mean(x * x, axis=-1, keepdims=True) + EPS)


def _silu(x):
    return x * (1.0 / (1.0 + jnp.exp(-x)))


def _emit_pipeline(n_items, stages):
    depth = max(off for off, _ in stages)
    for t in range(n_items + depth):
        for off, fn in stages:
            if 0 <= t - off < n_items:
                fn(t - off)


def _sub_blocks(tile_idx, n_tiles):
    n = ROW_TILE // SUB_ROWS
    blocks = []
    for r in range(n):
        rows = slice(r * SUB_ROWS, (r + 1) * SUB_ROWS)
        blocks.append((rows, (tile_idx == n_tiles - 1) if r == n - 1 else False))
    return blocks


def _mod_vec(mod_ref, batch, is_ctx, idx):
    row = batch if is_ctx is False else jnp.where(is_ctx, MOD_ROWS - 1, batch)
    return mod_ref[pl.ds(row, 1), idx * D_MODEL:(idx + 1) * D_MODEL]


def _prenorm(x, nw, mod_ref, batch, is_ctx, shift_idx):
    shift = _mod_vec(mod_ref, batch, is_ctx, shift_idx)
    scale = _mod_vec(mod_ref, batch, is_ctx, shift_idx + 1)
    return _rms(x) * (nw * (1.0 + scale)) + shift


def _postnorm_residual(x, y, nw, mod_ref, batch, is_ctx, gate_idx):
    gate = _mod_vec(mod_ref, batch, is_ctx, gate_idx)
    return x + _rms(y) * (gate * nw)


def _params(n_axes):
    return pltpu.CompilerParams(dimension_semantics=("arbitrary",) * n_axes,
                                vmem_limit_bytes=VMEM_LIMIT)


def _resident(a, layer=None):
    if layer is None:
        shape, index = a.shape, (0,) * a.ndim
    else:
        shape, index = (None,) + a.shape[1:], (layer,) + (0,) * (a.ndim - 1)
    return pl.BlockSpec(shape, lambda *_: index, pipeline_mode=pl.Buffered(1))


def _ada_kernel(c_ref, w_ref, b_ref, o_ref):
    a = _silu(c_ref[...])
    a_hi = a.astype(BF16)
    a_hi32 = a_hi.astype(F32)
    a_parts = jnp.concatenate([a_hi32, a - a_hi32], axis=0).astype(BF16)
    w_hi, w_lo = _bf16_parts(w_ref[0], 2)
    both = _dot(a_parts, w_hi)
    o_ref[0] = both[:MOD_ROWS] + both[MOD_ROWS:] + _dot(a_hi, w_lo) + b_ref[0]


def _ada_table(c_rows, ada_w, ada_b):
    depth, d, n = ada_w.shape
    tn = n // 2
    return pl.pallas_call(
        _ada_kernel,
        grid=(depth, n // tn),
        in_specs=[
            pl.BlockSpec((MOD_ROWS, d), lambda l, j: (0, 0)),
            pl.BlockSpec((1, d, tn), lambda l, j: (l, 0, j)),
            pl.BlockSpec((1, 1, tn), lambda l, j: (l, 0, j)),
        ],
        out_specs=pl.BlockSpec((1, MOD_ROWS, tn), lambda l, j: (l, 0, j)),
        out_shape=jax.ShapeDtypeStruct((depth, MOD_ROWS, n), F32),
        compiler_params=_params(2),
        name="ada_table",
    )(c_rows, ada_w, ada_b.reshape(depth, 1, n))


def _in_proj_kernel(*refs, joins, n_tiles):
    n_blk = ROW_TILE // SUB_ROWS
    x_refs, refs = refs[:n_blk], refs[n_blk:]
    ctx_ref = refs[0] if joins else None
    mod_ref, nw_ref, w_ref, q_ref, v_ref, zf_ref, zb_ref, g_ref = refs[joins:joins + 8]
    xs_ref = refs[-1] if joins else None
    b, i = pl.program_id(0), pl.program_id(1)
    d = D_MODEL
    blocks = _sub_blocks(i, n_tiles)
    hs = [None] * n_blk

    def pre(n):
        rows, is_ctx = blocks[n]
        x = x_refs[n][0]
        if joins:
            if is_ctx is not False:
                x = jnp.where(is_ctx, ctx_ref[0], x)
            xs_ref[0, rows] = x
        hs[n] = _prenorm(x, nw_ref[0:1, :], mod_ref, b, is_ctx, 0).astype(BF16)

    def project(n):
        rows, _ = blocks[n]
        h = hs[n]
        q_ref[0, rows] = (_dot(h, w_ref[:, 0:d]) * (HEAD_DIM ** -0.5)).astype(BF16)
        v_ref[0, rows] = _dot(h, w_ref[:, d:2 * d]).astype(BF16)
        zf_ref[0, rows] = _dot(h, w_ref[:, 2 * d:3 * d])
        zb_ref[0, rows] = _dot(h, w_ref[:, 3 * d:4 * d])
        g_ref[0, rows] = _dot(h, w_ref[:, 4 * d:5 * d]).astype(BF16)

    _emit_pipeline(n_blk, [(0, pre), (1, project)])


def _in_proj(x, ctx, mod_all, norm_w, w, layer):
    bsz, _, d = x.shape
    joins = ctx is not None
    s = x.shape[1] + (CTX_LEN if joins else 0)
    n_tiles = s // ROW_TILE
    n_blk = ROW_TILE // SUB_ROWS
    last_block = x.shape[1] // SUB_ROWS - 1
    tile = pl.BlockSpec((1, ROW_TILE, d), lambda b, i: (b, i, 0))

    def x_block(n):
        return pl.BlockSpec((1, SUB_ROWS, d), lambda b, i: (b, jnp.minimum(i * n_blk + n, last_block), 0))

    in_specs = [x_block(n) for n in range(n_blk)]
    in_specs += [pl.BlockSpec((1, CTX_LEN, d), lambda b, i: (b, 0, 0))] if joins else []
    in_specs += [_resident(mod_all, layer), _resident(norm_w, layer), _resident(w)]
    sds16 = jax.ShapeDtypeStruct((bsz, s, d), BF16)
    sds32 = jax.ShapeDtypeStruct((bsz, s, d), F32)
    out = pl.pallas_call(
        functools.partial(_in_proj_kernel, joins=joins, n_tiles=n_tiles),
        grid=(bsz, n_tiles),
        in_specs=in_specs,
        out_specs=[tile] * (6 if joins else 5),
        out_shape=[sds16, sds16, sds32, sds32, sds16] + ([sds32] if joins else []),
        compiler_params=_params(2),
        name="hgrn_in_proj",
    )(*([x] * n_blk), *([ctx] if joins else []), mod_all, norm_w, w)
    return (out[:5], out[5]) if joins else (out, x)


def _scan_levels():
    levels = []
    b = SCAN_CHUNK // 2
    while b >= SCAN_BASE:
        levels.append(b)
        b //= 2
    return levels


def _cumsum_matrix(forward):
    t = np.arange(SCAN_CHUNK)
    tri = t[None, :] <= t[:, None] if forward else t[None, :] >= t[:, None]
    return tri.astype(np.float32)


def _level_masks(forward):
    c = SCAN_CHUNK
    t = lax.broadcasted_iota(jnp.int32, (c, c), 0)
    s = lax.broadcasted_iota(jnp.int32, (c, c), 1)
    if not forward:
        t, s = (c - 1) - t, (c - 1) - s
    masks = []
    for b in _scan_levels():
        same = (t // (2 * b)) == (s // (2 * b))
        masks.append(same & (t % (2 * b) >= b) & (s % (2 * b) < b))
    same = (t // SCAN_BASE) == (s // SCAN_BASE)
    masks.append(same & (s <= t))
    return masks


def _pair_level(q, k, c, half, forward):
    packed_rows = 16
    qs, ks = [], []
    for start in range(0, SCAN_CHUNK, 2 * half):
        lo, hi = slice(start, start + half), slice(start + half, start + 2 * half)
        if forward:
            ref, q_rows, k_rows = c[start + half - 1:start + half], hi, lo
        else:
            ref, q_rows, k_rows = c[start + half:start + half + 1], lo, hi
        eq = jnp.exp2(c[q_rows] - ref)
        ek = jnp.exp2(ref - c[k_rows])
        if half % packed_rows == 0:
            zero = jnp.zeros((half, c.shape[-1]), BF16)
            qe, ke = (q[q_rows] * eq).astype(BF16), (k[k_rows] * ek).astype(BF16)
            qs += [zero, qe] if forward else [qe, zero]
            ks += [ke, zero] if forward else [zero, ke]
        else:
            e = jnp.concatenate([ek, eq] if forward else [eq, ek], axis=0)
            rows = slice(start, start + 2 * half)
            qs.append((q[rows] * e).astype(BF16))
            ks.append((k[rows] * e).astype(BF16))
    return jnp.concatenate(qs, axis=0), jnp.concatenate(ks, axis=0)


def _diag_level(q, k, c, forward):
    c3 = c.reshape(SCAN_CHUNK // SCAN_BASE, SCAN_BASE, c.shape[-1])
    r = SCAN_BASE // 2 - 1 if forward else SCAN_BASE // 2
    x = (c3 - c3[:, r:r + 1, :]).reshape(c.shape)
    return (q * jnp.exp2(x)).astype(BF16), (k * jnp.exp2(-x)).astype(BF16)


class _Unit:
    def __init__(self, q_ref, v_ref, z_ref, o_ref, st_ref, rows, h, tri, masks, forward):
        self.q_ref, self.v_ref, self.z_ref, self.o_ref, self.st_ref = q_ref, v_ref, z_ref, o_ref, st_ref
        self.rows, self.h, self.tri, self.masks, self.forward = rows, h, tri, masks, forward
        self.hs = slice(h * HEAD_DIM, (h + 1) * HEAD_DIM)


def _stage_gates(u, lb):
    lb = lb[:, u.hs]
    rest = 1.0 - lb
    z = u.z_ref[0, u.rows, u.hs]
    sig = 1.0 / (1.0 + jnp.exp2(z * (-1.0 / np.log(2.0))))
    gate = rest * sig
    g = jnp.log(jnp.maximum(lb + gate, F_FLOOR)) * (1.0 / np.log(2.0))
    u.k = rest - gate
    u.q = u.q_ref[0, u.rows, u.hs].astype(F32)
    u.v16 = u.v_ref[0, u.rows, u.hs]
    u.cs = _dot(u.tri, jnp.concatenate(_bf16_parts(g, CUMSUM_PARTS), axis=-1))


def _stage_levels(u):
    c = u.cs[:, :HEAD_DIM]
    for i in range(1, CUMSUM_PARTS):
        c = c + u.cs[:, i * HEAD_DIM:(i + 1) * HEAD_DIM]
    last = SCAN_CHUNK - 1 if u.forward else 0
    c_last = c[last:last + 1]
    u.q_in = (u.q * jnp.exp2(c)).astype(BF16)
    u.k_st = (u.k * jnp.exp2(c_last - c)).astype(BF16)
    u.e_last = jnp.exp2(c_last)
    u.levels = [_pair_level(u.q, u.k, c, b, u.forward) for b in _scan_levels()]
    u.levels.append(_diag_level(u.q, u.k, c, u.forward))


def _stage_scores(u):
    a = None
    for li, (qd, kd) in enumerate(u.levels):
        a_l = lax.dot_general(qd, kd, (((1,), (1,)), ((), ())), preferred_element_type=F32)
        a = a_l if li == 0 else jnp.where(u.masks[li], a_l, a)
    u.a16 = a.astype(BF16)


def _stage_output(u):
    contract_lanes = (((1,), (1,)), ((), ()))
    contract_rows = (((0,), (0,)), ((), ()))
    st = u.st_ref[u.h]
    o = _dot(u.a16, u.v16)
    o = o + lax.dot_general(u.q_in, st.astype(BF16), contract_lanes, preferred_element_type=F32)
    u.o_ref[0, u.rows, u.hs] = o.astype(BF16)
    u.st_ref[u.h] = st * u.e_last + lax.dot_general(u.v16, u.k_st, contract_rows,
                                                    preferred_element_type=F32)


def _scan_units(fwd_refs, bwd_refs, tri_f, tri_b):
    n_chunks = SCAN_BLOCK // SCAN_CHUNK
    masks = {True: _level_masks(True), False: _level_masks(False)}
    units = []
    for ci in range(n_chunks):
        for refs, tri, forward in ((fwd_refs, tri_f, True), (bwd_refs, tri_b, False)):
            c = ci if forward else n_chunks - 1 - ci
            rows = slice(c * SCAN_CHUNK, (c + 1) * SCAN_CHUNK)
            for h in range(HEADS):
                units.append(_Unit(*refs, rows, h, tri, masks[forward], forward))
    return units


def _scan_kernel(*refs, layer, n_casts):
    lb_ref, mf_ref, mb_ref, qf_ref, vf_ref, zf_ref, qb_ref, vb_ref, zb_ref = refs[:9]
    cast_in = refs[9:9 + n_casts]
    of_ref, ob_ref = refs[9 + n_casts:11 + n_casts]
    cast_out = refs[11 + n_casts:11 + 2 * n_casts]
    sf_ref, sb_ref = refs[11 + 2 * n_casts:]

    @pl.when(pl.program_id(1) == 0)
    def _():
        sf_ref[...] = jnp.zeros_like(sf_ref)
        sb_ref[...] = jnp.zeros_like(sb_ref)

    step = pl.program_id(0) * pl.num_programs(1) + pl.program_id(1)

    @pl.when(step % CAST_EVERY == 0)
    def _():
        for src_ref, dst_ref in zip(cast_in, cast_out):
            dst_ref[...] = src_ref[...].astype(BF16)

    raw = lb_ref[...]
    ex = jnp.exp(raw - jnp.max(raw, axis=0, keepdims=True))
    soft = ex / jnp.sum(ex, axis=0, keepdims=True)
    lb = jnp.zeros((1, D_MODEL), F32)
    for i in range(1, layer + 1):
        lb = lb + soft[i:i + 1]

    units = _scan_units((qf_ref, vf_ref, zf_ref, of_ref, sf_ref), (qb_ref, vb_ref, zb_ref, ob_ref, sb_ref),
                        mf_ref[...], mb_ref[...])
    stages = (functools.partial(_stage_gates, lb=lb), _stage_levels, _stage_scores, _stage_output)
    _emit_pipeline(len(units), [(off, lambda n, fn=fn: fn(units[n])) for off, fn in zip(SCAN_SKEW, stages)])


def _cast_block_rows(rows, n_blocks):
    packed_rows = 16
    for block in range(packed_rows, rows + 1, packed_rows):
        if rows % block == 0 and rows // block <= n_blocks:
            return block
    raise ValueError(f"cannot spread {rows} rows over {n_blocks} blocks")


def _scan(q, v, zf, zb, hgrn_lb, layer, casts):
    bsz, s, _ = q.shape
    c = SCAN_BLOCK
    n_steps = s // c
    n_ctx = CTX_LEN // c
    mf = jnp.asarray(_cumsum_matrix(True), BF16)
    mb = jnp.asarray(_cumsum_matrix(False), BF16)

    def fwd_block(i):
        return jnp.where(i < n_ctx, n_steps - n_ctx + i, i - n_ctx)

    def bwd_block(i):
        return n_steps - 1 - i

    def rows(block_of):
        return pl.BlockSpec((1, c, D_MODEL), lambda b, i: (b, block_of(i), 0))

    fwd, bwd = rows(fwd_block), rows(bwd_block)
    out_sds = jax.ShapeDtypeStruct((bsz, s, D_MODEL), BF16)

    cast_in, cast_out, cast_sds = [], [], []
    for w, row0, n_rows in casts:
        block = _cast_block_rows(n_rows, bsz * n_steps // CAST_EVERY)
        first, last = row0 // block, n_rows // block - 1

        def step_block(b, i, last=last):
            return jnp.minimum((b * n_steps + i) // CAST_EVERY, last)

        cast_in.append(pl.BlockSpec((block, w.shape[1]),
                                    lambda b, i, f=first, sb=step_block: (f + sb(b, i), 0)))
        cast_out.append(pl.BlockSpec((block, w.shape[1]), lambda b, i, sb=step_block: (sb(b, i), 0)))
        cast_sds.append(jax.ShapeDtypeStruct((n_rows, w.shape[1]), BF16))

    out = pl.pallas_call(
        functools.partial(_scan_kernel, layer=layer, n_casts=len(casts)),
        grid=(bsz, n_steps),
        in_specs=[_resident(hgrn_lb), _resident(mf), _resident(mb), fwd, fwd, fwd, bwd, bwd, bwd] + cast_in,
        out_specs=[fwd, bwd] + cast_out,
        out_shape=[out_sds, out_sds] + cast_sds,
        scratch_shapes=[pltpu.VMEM((HEADS, HEAD_DIM, HEAD_DIM), F32),
                        pltpu.VMEM((HEADS, HEAD_DIM, HEAD_DIM), F32)],
        compiler_params=_params(2),
        name="hgrn_scan",
    )(hgrn_lb, mf, mb, q, v, zf, q, v, zb, *[w for w, _, _ in casts])
    return out[0], out[1], out[2:]


def _readout_ffn_kernel(x_ref, of_ref, ob_ref, g_ref, mod_ref, nw_ref, gn_ref, w_ref, wi_ref, wo_ref,
                        o_ref, *, n_tiles):
    b, i = pl.program_id(0), pl.program_id(1)
    tail = _FfnTail(x_ref, o_ref, mod_ref, nw_ref, wi_ref, wo_ref, b, _sub_blocks(i, n_tiles))
    mixed = [None] * len(tail.blocks)

    def mix(r):
        rows, _ = tail.blocks[r]
        o = of_ref[0, rows].astype(F32) + ob_ref[0, rows].astype(F32)
        gn = gn_ref[...]
        heads = [_rms(o[:, h * HEAD_DIM:(h + 1) * HEAD_DIM]) * gn for h in range(HEADS)]
        y = jnp.concatenate(heads, axis=-1) * _silu(g_ref[0, rows].astype(F32))
        mixed[r] = y.astype(BF16)

    def project_out(r):
        tail.ys[r] = _dot(mixed[r], w_ref[...])

    tail.emit([mix, project_out])


def _readout_ffn(xs, o_f, o_b, g, mod_all, norm_w, gnorm_all, w_out_all, ffn_w_in, ffn_w_out,
                 layer, j, ffn_idx, out_rows):
    bsz, s, d = xs.shape
    n_tiles = s // ROW_TILE
    tile = pl.BlockSpec((1, ROW_TILE, d), lambda b, i: (b, i, 0))
    return pl.pallas_call(
        functools.partial(_readout_ffn_kernel, n_tiles=n_tiles),
        grid=(bsz, n_tiles),
        in_specs=[tile, tile, tile, tile, _resident(mod_all, layer), _resident(norm_w, layer),
                  _resident(gnorm_all, j), _resident(w_out_all, j),
                  _resident(ffn_w_in, ffn_idx), _resident(ffn_w_out, ffn_idx)],
        out_specs=tile,
        out_shape=jax.ShapeDtypeStruct((bsz, out_rows, d), F32),
        compiler_params=_params(2),
        name="hgrn_readout_ffn",
    )(xs, o_f, o_b, g, mod_all, norm_w, gnorm_all, w_out_all, ffn_w_in, ffn_w_out)


def _conv_ffn_kernel(x_ref, mod_ref, nw_ref, ci_ref, cw_ref, co_ref, wi_ref, wo_ref, o_ref,
                     *, n_tiles, seq):
    b, i = pl.program_id(0), pl.program_id(1)
    d = D_MODEL
    tail = _FfnTail(x_ref, o_ref, mod_ref, nw_ref, wi_ref, wo_ref, b, _sub_blocks(i, n_tiles))
    n_blk = len(tail.blocks)
    hs, us, gs, ts = ([None] * n_blk for _ in range(4))

    def pre(r):
        rows, is_ctx = tail.blocks[r]
        hs[r] = _prenorm(x_ref[0, rows], nw_ref[0:1, :], mod_ref, b, is_ctx, 0).astype(BF16)

    def project(r):
        gs[r] = _dot(hs[r], ci_ref[:, 0:d])
        us[r] = _dot(hs[r], ci_ref[:, d:2 * d]) * _dot(hs[r], ci_ref[:, 2 * d:3 * d])

    def convolve(r):
        rows, _ = tail.blocks[r]
        u = us[r]
        pos = i * ROW_TILE + rows.start + lax.broadcasted_iota(jnp.int32, (SUB_ROWS, 1), 0)
        in_ctx = pos >= seq
        has_prev = (pos != seq) & (in_ctx | (pos % GRID_W != 0))
        has_next = (pos != seq + CTX_LEN - 1) & (in_ctx | (pos % GRID_W != GRID_W - 1))
        u_prev = jnp.where(has_prev, pltpu.roll(u, 1, axis=0), 0.0)
        u_next = jnp.where(has_next, pltpu.roll(u, SUB_ROWS - 1, axis=0), 0.0)
        y = u_prev * cw_ref[0:1, :] + u * cw_ref[1:2, :] + u_next * cw_ref[2:3, :]
        ts[r] = (gs[r] * y).astype(BF16)

    def project_out(r):
        tail.ys[r] = _dot(ts[r], co_ref[...])

    tail.emit([pre, project, convolve, project_out])


def _conv_ffn(xs, mod_all, norm_w, w_in, conv_w_all, w_out_all, ffn_w_in, ffn_w_out,
              layer, j, ffn_idx, out_rows):
    bsz, s, d = xs.shape
    n_tiles = s // ROW_TILE
    tile = pl.BlockSpec((1, ROW_TILE, d), lambda b, i: (b, i, 0))
    return pl.pallas_call(
        functools.partial(_conv_ffn_kernel, n_tiles=n_tiles, seq=s - CTX_LEN),
        grid=(bsz, n_tiles),
        in_specs=[tile, _resident(mod_all, layer), _resident(norm_w, layer), _resident(w_in, 0),
                  _resident(conv_w_all, j), _resident(w_out_all, j),
                  _resident(ffn_w_in, ffn_idx), _resident(ffn_w_out, ffn_idx)],
        out_specs=tile,
        out_shape=jax.ShapeDtypeStruct((bsz, out_rows, d), F32),
        compiler_params=_params(2),
        name="conv_ffn",
    )(xs, mod_all, norm_w, w_in, conv_w_all, w_out_all, ffn_w_in, ffn_w_out)


class _FfnTail:
    def __init__(self, x_ref, o_ref, mod_ref, nw_ref, wi_ref, wo_ref, batch, blocks):
        self.x_ref, self.o_ref, self.mod_ref, self.nw_ref = x_ref, o_ref, mod_ref, nw_ref
        self.wi_ref, self.wo_ref, self.batch, self.blocks = wi_ref, wo_ref, batch, blocks
        self.hidden = wo_ref.shape[0]
        self.tf = self.hidden // FFN_SPLIT
        n_blk = len(blocks)
        self.ys, self.x1, self.hs, self.accs = ([None] * n_blk for _ in range(4))
        self.gates, self.ups, self.acts = ([None] * (n_blk * FFN_SPLIT) for _ in range(3))

    def residual(self, r):
        rows, is_ctx = self.blocks[r]
        x1 = _postnorm_residual(self.x_ref[0, rows], self.ys[r], self.nw_ref[1:2, :],
                                self.mod_ref, self.batch, is_ctx, 2)
        self.x1[r] = x1
        self.hs[r] = _prenorm(x1, self.nw_ref[2:3, :], self.mod_ref, self.batch, is_ctx, 3).astype(BF16)

    def project(self, n):
        r, f = divmod(n, FFN_SPLIT)
        lo, hi = f * self.tf, (f + 1) * self.tf
        self.gates[n] = _dot(self.hs[r], self.wi_ref[:, lo:hi])
        self.ups[n] = _dot(self.hs[r], self.wi_ref[:, self.hidden + lo:self.hidden + hi])

    def activate(self, n):
        self.acts[n] = (_silu(self.gates[n]) * self.ups[n]).astype(BF16)
        self.gates[n] = self.ups[n] = None

    def down(self, n):
        r, f = divmod(n, FFN_SPLIT)
        y = _dot(self.acts[n], self.wo_ref[f * self.tf:(f + 1) * self.tf, :])
        self.accs[r] = y if f == 0 else self.accs[r] + y
        self.acts[n] = None

    def finish(self, r):
        rows, is_ctx = self.blocks[r]
        self.o_ref[0, rows] = _postnorm_residual(self.x1[r], self.accs[r], self.nw_ref[3:4, :],
                                                 self.mod_ref, self.batch, is_ctx, 5)

    def emit(self, mixer_stages):
        def on_first(fn):
            return lambda n: fn(n // FFN_SPLIT) if n % FFN_SPLIT == 0 else None

        def on_last(fn):
            return lambda n: fn(n // FFN_SPLIT) if n % FFN_SPLIT == FFN_SPLIT - 1 else None

        stages = [on_first(fn) for fn in mixer_stages]
        stages += [on_first(self.residual), self.project, self.activate, self.down, on_last(self.finish)]
        _emit_pipeline(len(self.blocks) * FFN_SPLIT, list(enumerate(stages)))


def kernel(x, c, ctx, c_ctx, ada_w, ada_b, norm_w, hgrn_w_in, hgrn_w_out, hgrn_gnorm, hgrn_lb,
           conv_w_in, conv_w, conv_w_out, ffn_w_in, ffn_w_out):
    bsz, seq, d = x.shape
    depth = ada_w.shape[0]
    s_all = seq + CTX_LEN
    assert d == D_MODEL and ctx.shape[1] == CTX_LEN and bsz < MOD_ROWS and depth % 2 == 0
    assert s_all % ROW_TILE == 0 and ROW_TILE % SUB_ROWS == 0 and SUB_ROWS % GRID_W == 0
    assert seq % SCAN_BLOCK == 0 and CTX_LEN % SCAN_BLOCK == 0 and seq % GRID_W == 0
    assert conv_w.shape[1] == 3, "the conv mixer kernel is written for three taps"

    c_rows = jnp.zeros((MOD_ROWS, d), F32).at[:bsz].set(c).at[MOD_ROWS - 1].set(c_ctx)
    mod_all = _ada_table(c_rows, ada_w, ada_b)

    d_ffn = ffn_w_in.shape[2]
    hidden = ffn_w_out.shape[1]
    lanes = 128
    assert hidden % (FFN_SPLIT * lanes) == 0 and d_ffn == 2 * hidden
    n_rec = hgrn_w_in.shape[0]
    w_in_first = hgrn_w_in[0].astype(BF16)
    hgrn_w_out, conv_w_out = hgrn_w_out.astype(BF16), conv_w_out.astype(BF16)
    hgrn_w_in_rows = hgrn_w_in.reshape(n_rec * d, -1)
    conv_w_in_rows = conv_w_in.reshape(-1, conv_w_in.shape[2])
    ffn_w_in_rows = ffn_w_in.reshape(depth * d, d_ffn)
    ffn_w_out_rows = ffn_w_out.reshape(depth * hidden, d)
    gnorm = hgrn_gnorm.reshape(-1, 1, HEAD_DIM)

    xs, w_in_next = None, w_in_first
    for j in range(n_rec):
        l = 2 * j
        first = xs is None
        (q, v, zf, zb, g), xs = _in_proj(x if first else xs, ctx if first else None,
                                         mod_all, norm_w, w_in_next, l)
        casts = [(ffn_w_in_rows, l * d, 2 * d), (ffn_w_out_rows, l * hidden, 2 * hidden),
                 (conv_w_in_rows, j * d, d)]
        if j + 1 < n_rec:
            casts.append((hgrn_w_in_rows, (j + 1) * d, d))
        o_f, o_b, cast = _scan(q, v, zf, zb, hgrn_lb, j, casts)
        ffn_in = cast[0].reshape(2, d, d_ffn)
        ffn_out = cast[1].reshape(2, hidden, d)
        conv_in = cast[2].reshape(1, d, -1)
        w_in_next = cast[3] if j + 1 < n_rec else None
        xs = _readout_ffn(xs, o_f, o_b, g, mod_all, norm_w, gnorm, hgrn_w_out, ffn_in, ffn_out,
                          l, j, 0, s_all)
        xs = _conv_ffn(xs, mod_all, norm_w, conv_in, conv_w, conv_w_out, ffn_in, ffn_out,
                       l + 1, j, 1, seq if l + 1 == depth - 1 else s_all)
    return xs
```
